```python
import math
import jax, jax.numpy as jnp
from jax import lax
import numpy as np

D_MODEL = 1024
BATCH = 2
SEQ = 8192
DEPTH = 1

ATTN_HEADS = 8
ATTN_QK_DIM = 64
ATTN_V_DIM = 2 * ATTN_QK_DIM
ATTN_QK_COLS = ATTN_HEADS * 2 * ATTN_QK_DIM
ATTN_WIDTH = ATTN_HEADS * ATTN_V_DIM
ROPE_DIM = ATTN_QK_DIM // 4
ROPE_THETA = 500000.0
Q_BLOCK = 128
SSD_INNER = 2 * D_MODEL
SSD_HEAD_DIM = 64
SSD_HEADS = SSD_INNER // SSD_HEAD_DIM
SSD_GROUPS = 4
SSD_STATE = 128
SSD_CONV = 4
SSD_CHUNK = 128
SSD_CONV_CH = SSD_INNER + 2 * SSD_GROUPS * SSD_STATE
N_EXPERTS = 256
TOP_K = 8
N_EXPERT_GROUPS = 8
TOPK_GROUPS = 4
EXPERT_FF = 256
SHARED_FF = 256
ROUTED_SCALE = 2.5
MOE_BLOCK = 128
NORM_EPS = 1e-6
IN_SIZES = (ATTN_QK_COLS, ATTN_QK_COLS, ATTN_WIDTH, SSD_INNER, SSD_CONV_CH, SSD_HEADS, 2 * D_MODEL)
IN_COLS = int(sum(IN_SIZES))
IN_SPLITS = [int(s) for s in np.cumsum(IN_SIZES)[:-1]]

kernel_name = 'hybrid_diffattn_ssd_moe_block'


def rms_norm(x, gain):
    xf = x.astype(jnp.float32)
    y = xf * lax.rsqrt(jnp.mean(xf * xf, axis=-1, keepdims=True) + NORM_EPS)
    return (y * gain.astype(jnp.float32)).astype(x.dtype)


def partial_rope(t, positions):
    half = ROPE_DIM // 2
    inv_freq = 1.0 / (ROPE_THETA ** (jnp.arange(0, ROPE_DIM, 2, dtype=jnp.float32) / ROPE_DIM))
    ang = positions.astype(jnp.float32)[..., None] * inv_freq
    cos = jnp.cos(ang)[:, :, None, None, :]
    sin = jnp.sin(ang)[:, :, None, None, :]
    r1 = t[..., :half].astype(jnp.float32)
    r2 = t[..., half:ROPE_DIM].astype(jnp.float32)
    rot = jnp.concatenate([r1 * cos - r2 * sin, r2 * cos + r1 * sin], axis=-1).astype(t.dtype)
    return jnp.concatenate([rot, t[..., ROPE_DIM:]], axis=-1)


def diff_attention(q, k, v, lam, head_gain, lambda_init):
    bsz, seq = q.shape[0], q.shape[1]
    nq = seq // Q_BLOCK
    scale = ATTN_QK_DIM ** -0.5
    qb = jnp.moveaxis(q.reshape(bsz, nq, Q_BLOCK, ATTN_HEADS, 2, ATTN_QK_DIM), 1, 0)
    key_pos = jnp.arange(seq)

    def one_block(args):
        q_blk, blk = args
        s = jnp.einsum('bqhcd,bkhcd->bhcqk', q_blk, k, preferred_element_type=jnp.float32) * scale
        q_pos = blk * Q_BLOCK + jnp.arange(Q_BLOCK)
        mask = key_pos[None, :] <= q_pos[:, None]
        p = jax.nn.softmax(jnp.where(mask, s, -jnp.inf), axis=-1)
        a = p[:, :, 0] - lam * p[:, :, 1]
        return jnp.einsum('bhqk,bkhd->bqhd', a.astype(v.dtype), v)

    o = lax.map(one_block, (qb, jnp.arange(nq)))
    o = jnp.moveaxis(o, 0, 1).reshape(bsz, seq, ATTN_HEADS, ATTN_V_DIM)
    o = rms_norm(o, head_gain) * (1.0 - lambda_init)
    return o.reshape(bsz, seq, ATTN_WIDTH)


def causal_depthwise_conv(u, w, b):
    out = lax.conv_general_dilated(u, w[:, None, :], window_strides=(1,), padding=[(SSD_CONV - 1, 0)],
                                   dimension_numbers=('NWC', 'WIO', 'NWC'), feature_group_count=u.shape[-1])
    return out + b


def ssd_scan(xh, dt, a_neg, b_in, c_in):
    bsz, seq = xh.shape[0], xh.shape[1]
    nc = seq // SSD_CHUNK
    e = SSD_HEADS // SSD_GROUPS

    def chunks(t):
        return jnp.moveaxis(t.reshape(bsz, nc, SSD_CHUNK, *t.shape[2:]), 1, 0)

    xdt = (xh.astype(jnp.float32) * dt[..., None]).reshape(bsz, seq, SSD_GROUPS, e, SSD_HEAD_DIM)
    a = (dt * a_neg).reshape(bsz, seq, SSD_GROUPS, e)
    causal = jnp.tril(jnp.ones((SSD_CHUNK, SSD_CHUNK), dtype=bool))

    def step(state, inp):
        x_c, a_c, b_c, c_c = inp
        a_cum = lax.cumsum(a_c, axis=1)
        seg = a_cum[:, :, None] - a_cum[:, None, :]
        decay = jnp.exp(jnp.where(causal[None, :, :, None, None], seg, -jnp.inf))
        cb = jnp.einsum('blgn,bsgn->blsg', c_c, b_c)
        y_diag = jnp.einsum('blsg,blsge,bsgep->blgep', cb, decay, x_c)
        y_off = jnp.einsum('blgn,bgepn,blge->blgep', c_c, state, jnp.exp(a_cum))
        to_end = jnp.exp(a_cum[:, -1:] - a_cum)
        new_state = (state * jnp.exp(a_cum[:, -1])[..., None, None]
                     + jnp.einsum('blgn,blge,blgep->bgepn', b_c, to_end, x_c))
        return new_state, y_diag + y_off

    state0 = jnp.zeros((bsz, SSD_GROUPS, e, SSD_HEAD_DIM, SSD_STATE), jnp.float32)
    _, y = lax.scan(step, state0, (chunks(xdt), chunks(a), chunks(b_in.astype(jnp.float32)),
                                   chunks(c_in.astype(jnp.float32))))
    return jnp.moveaxis(y, 0, 1).reshape(bsz, seq, SSD_HEADS, SSD_HEAD_DIM)


def route(h, w_router, router_bias):
    n = h.shape[0]
    scores = jax.nn.sigmoid(jnp.einsum('nd,de->ne', h, w_router, preferred_element_type=jnp.float32))
    choice = scores + router_bias.astype(jnp.float32)
    per_group = N_EXPERTS // N_EXPERT_GROUPS
    group_score = lax.top_k(choice.reshape(n, N_EXPERT_GROUPS, per_group), 2)[0].sum(-1)
    _, top_groups = lax.top_k(group_score, TOPK_GROUPS)
    group_mask = jnp.any(top_groups[..., None] == jnp.arange(N_EXPERT_GROUPS), axis=-2)
    masked = jnp.where(jnp.repeat(group_mask, per_group, axis=-1), choice, -jnp.inf)
    _, idx = lax.top_k(masked, TOP_K)
    w = jnp.take_along_axis(scores, idx, axis=-1)
    w = w / (jnp.sum(w, axis=-1, keepdims=True) + 1e-20) * ROUTED_SCALE
    return idx, w


def routed_experts(h, idx, wts, w_gate, w_up, w_down):
    n, d = h.shape
    n_pairs = n * TOP_K
    n_blocks = -(-n_pairs // MOE_BLOCK) + N_EXPERTS
    flat_e = idx.reshape(-1)
    flat_t = jnp.arange(n_pairs, dtype=jnp.int32) // TOP_K
    flat_w = wts.reshape(-1)
    order = jnp.argsort(flat_e)
    e_sorted = flat_e[order]
    counts = jnp.bincount(flat_e, length=N_EXPERTS)
    padded = (counts + MOE_BLOCK - 1) // MOE_BLOCK * MOE_BLOCK
    padded_end = jnp.cumsum(padded)
    padded_start = padded_end - padded
    start = jnp.cumsum(counts) - counts
    dest = padded_start[e_sorted] + jnp.arange(n_pairs) - start[e_sorted]
    slots = n_blocks * MOE_BLOCK
    slot_tok = jnp.full((slots,), n, jnp.int32).at[dest].set(flat_t[order])
    slot_w = jnp.zeros((slots,), h.dtype).at[dest].set(flat_w[order].astype(h.dtype))
    block_e = jnp.minimum(jnp.searchsorted(padded_end, jnp.arange(n_blocks) * MOE_BLOCK, side='right'),
                          N_EXPERTS - 1)
    h_pad = jnp.concatenate([h, jnp.zeros((1, d), h.dtype)], axis=0)

    def step(acc, inp):
        tok, wt, e = inp
        xb = h_pad[tok]
        y = (jax.nn.silu(xb @ w_gate[e]) * (xb @ w_up[e])) @ w_down[e]
        return acc.at[tok].add(y * wt[:, None]), None

    acc, _ = lax.scan(step, jnp.zeros((n + 1, d), h.dtype),
                      (slot_tok.reshape(n_blocks, MOE_BLOCK), slot_w.reshape(n_blocks, MOE_BLOCK), block_e))
    return acc[:n]


def swiglu(h, wg, wu, wd):
    return (jax.nn.silu(h @ wg) * (h @ wu)) @ wd


def setup_inputs(seed: int = 0) -> dict:
    key = jax.random.key(seed)
    ks = jax.random.split(key, 32)
    f32 = jnp.float32
    L = DEPTH

    def nrm(k, shape, scale):
        return jax.random.normal(k, shape, f32) * scale

    dt0 = jnp.exp(jax.random.uniform(ks[14], (L, SSD_HEADS), f32, math.log(1e-3), math.log(1e-1)))
    return {
        'x': nrm(ks[0], (BATCH, SEQ, D_MODEL), 1.0),
        'c': nrm(ks[1], (BATCH, D_MODEL), 1.0),
        'positions': jnp.broadcast_to(jnp.arange(SEQ, dtype=jnp.int32), (BATCH, SEQ)),
        'w_ada': nrm(ks[2], (L, D_MODEL, 6 * D_MODEL), 0.5 * D_MODEL ** -0.5),
        'b_ada': nrm(ks[3], (L, 6 * D_MODEL), 0.02),
        'norm_mix': 1.0 + nrm(ks[4], (L, D_MODEL), 0.02),
        'w_in': nrm(ks[5], (L, D_MODEL, IN_COLS), D_MODEL ** -0.5),
        'b_gate': nrm(ks[6], (L, 2 * D_MODEL), 0.02),
        'lambda_q1': nrm(ks[7], (L, ATTN_QK_DIM), 0.1),
        'lambda_k1': nrm(ks[8], (L, ATTN_QK_DIM), 0.1),
        'lambda_q2': nrm(ks[9], (L, ATTN_QK_DIM), 0.1),
        'lambda_k2': nrm(ks[10], (L, ATTN_QK_DIM), 0.1),
        'attn_head_norm': 1.0 + nrm(ks[11], (L, ATTN_V_DIM), 0.02),
        'conv_w': nrm(ks[12], (L, SSD_CONV, SSD_CONV_CH), SSD_CONV ** -0.5),
        'conv_b': nrm(ks[13], (L, SSD_CONV_CH), 0.02),
        'dt_bias': dt0 + jnp.log(-jnp.expm1(-dt0)),
        'a_log': jnp.log(jax.random.uniform(ks[15], (L, SSD_HEADS), f32, 1.0, 16.0)),
        'd_skip': 1.0 + nrm(ks[16], (L, SSD_HEADS), 0.02),
        'ssd_norm': 1.0 + nrm(ks[17], (L, SSD_INNER), 0.02),
        'w_branch_attn': nrm(ks[18], (L, ATTN_WIDTH, D_MODEL), ATTN_WIDTH ** -0.5),
        'w_branch_ssd': nrm(ks[19], (L, SSD_INNER, D_MODEL), SSD_INNER ** -0.5),
        'w_out': nrm(ks[20], (L, D_MODEL, D_MODEL), D_MODEL ** -0.5),
        'norm_ffn': 1.0 + nrm(ks[21], (L, D_MODEL), 0.02),
        'w_router': nrm(ks[22], (L, D_MODEL, N_EXPERTS), D_MODEL ** -0.5),
        'router_bias': nrm(ks[23], (L, N_EXPERTS), 0.01),
        'w_exp_gate': nrm(ks[24], (L, N_EXPERTS, D_MODEL, EXPERT_FF), D_MODEL ** -0.5),
        'w_exp_up': nrm(ks[25], (L, N_EXPERTS, D_MODEL, EXPERT_FF), D_MODEL ** -0.5),
        'w_exp_down': nrm(ks[26], (L, N_EXPERTS, EXPERT_FF, D_MODEL), EXPERT_FF ** -0.5),
        'w_sh_gate': nrm(ks[27], (L, D_MODEL, SHARED_FF), D_MODEL ** -0.5),
        'w_sh_up': nrm(ks[28], (L, D_MODEL, SHARED_FF), D_MODEL ** -0.5),
        'w_sh_down': nrm(ks[29], (L, SHARED_FF, D_MODEL), SHARED_FF ** -0.5),
        'norm_final': 1.0 + nrm(ks[30], (D_MODEL,), 0.02),
    }


def reference(x, c, positions, w_ada, b_ada, norm_mix, w_in, b_gate, lambda_q1, lambda_k1, lambda_q2,
              lambda_k2, attn_head_norm, conv_w, conv_b, dt_bias, a_log, d_skip, ssd_norm, w_branch_attn,
              w_branch_ssd, w_out, norm_ffn, w_router, router_bias, w_exp_gate, w_exp_up, w_exp_down,
              w_sh_gate, w_sh_up, w_sh_down, norm_final):
    bsz, seq, d = x.shape
    for l in range(DEPTH):
        mod = jax.nn.silu(c) @ w_ada[l] + b_ada[l]
        shift_m, scale_m, gate_m, shift_f, scale_f, gate_f = jnp.split(mod[:, None, :], 6, axis=-1)

        h = rms_norm(x, norm_mix[l]) * (1.0 + scale_m) + shift_m
        proj = h @ w_in[l]
        q, k, v, z, xbc, dt_raw, gates = jnp.split(proj, IN_SPLITS, axis=-1)

        q = partial_rope(q.reshape(bsz, seq, ATTN_HEADS, 2, ATTN_QK_DIM), positions)
        k = partial_rope(k.reshape(bsz, seq, ATTN_HEADS, 2, ATTN_QK_DIM), positions)
        v = v.reshape(bsz, seq, ATTN_HEADS, ATTN_V_DIM)
        lambda_init = 0.8 - 0.6 * math.exp(-0.3 * l)
        lam = (jnp.exp(jnp.sum(lambda_q1[l].astype(jnp.float32) * lambda_k1[l].astype(jnp.float32)))
               - jnp.exp(jnp.sum(lambda_q2[l].astype(jnp.float32) * lambda_k2[l].astype(jnp.float32)))
               + lambda_init)
        y_attn = diff_attention(q, k, v, lam, attn_head_norm[l], lambda_init)

        xbc = jax.nn.silu(causal_depthwise_conv(xbc, conv_w[l], conv_b[l]))
        xs, b_in, c_in = jnp.split(xbc, [SSD_INNER, SSD_INNER + SSD_GROUPS * SSD_STATE], axis=-1)
        dt = jax.nn.softplus(dt_raw.astype(jnp.float32) + dt_bias[l].astype(jnp.float32))
        a_neg = -jnp.exp(a_log[l].astype(jnp.float32))
        xh = xs.reshape(bsz, seq, SSD_HEADS, SSD_HEAD_DIM)
        y = ssd_scan(xh, dt, a_neg, b_in.reshape(bsz, seq, SSD_GROUPS, SSD_STATE),
                     c_in.reshape(bsz, seq, SSD_GROUPS, SSD_STATE))
        y = y + d_skip[l].astype(jnp.float32)[:, None] * xh.astype(jnp.float32)
        y = y.reshape(bsz, seq, SSD_INNER) * jax.nn.silu(z.astype(jnp.float32))
        y = rms_norm(y.reshape(bsz, seq, SSD_GROUPS, SSD_INNER // SSD_GROUPS),
                     ssd_norm[l].reshape(SSD_GROUPS, SSD_INNER // SSD_GROUPS))
        y_ssd = y.reshape(bsz, seq, SSD_INNER).astype(x.dtype)

        g_attn, g_ssd = jnp.split(jax.nn.sigmoid(gates + b_gate[l]), 2, axis=-1)
        mixed = g_attn * (y_attn @ w_branch_attn[l]) + g_ssd * (y_ssd @ w_branch_ssd[l])
        x = x + gate_m * (mixed @ w_out[l])

        h2 = rms_norm(x, norm_ffn[l]) * (1.0 + scale_f) + shift_f
        flat = h2.reshape(bsz * seq, d)
        idx, wts = route(flat, w_router[l], router_bias[l])
        routed = routed_experts(flat, idx, wts, w_exp_gate[l], w_exp_up[l], w_exp_down[l])
        shared = swiglu(flat, w_sh_gate[l], w_sh_up[l], w_sh_down[l])
        x = x + gate_f * (routed + shared).reshape(bsz, seq, d)
    return rms_norm(x, norm_final)
```

```python
import functools
import math

import jax
import jax.numpy as jnp
from jax import lax
from jax.experimental import pallas as pl
from jax.experimental.pallas import tpu as pltpu

F32 = jnp.float32
BF16 = jnp.bfloat16
I32 = jnp.int32

ATTN_HEADS = 8
ATTN_QK_DIM = 64
ATTN_V_DIM = 128
ROPE_DIM = 16
ROPE_THETA = 500000.0
SSD_HEAD_DIM = 64
SSD_GROUPS = 4
SSD_STATE = 128
SSD_CONV = 4
SSD_CHUNK = 128
N_EXPERTS = 256
TOP_K = 8
N_EXPERT_GROUPS = 8
TOPK_GROUPS = 4
ROUTED_SCALE = 2.5
NORM_EPS = 1e-6
LOG2E = 1.4426950408889634
NEG_BIG = -1e30

LANES = 128
SUBLANES = 8
VMEM_LIMIT_BYTES = 56 * 1024 * 1024

TOKEN_TILE = 256
MOE_BLOCK = 256
COMBINE_TILE = 128


def _cparams(sem, vmem=VMEM_LIMIT_BYTES):
    return pltpu.CompilerParams(dimension_semantics=sem, vmem_limit_bytes=vmem)


def _dot(a, b):
    return jnp.dot(a, b, preferred_element_type=F32)


def _dot_nt(a, b):
    return lax.dot_general(a, b, (((1,), (1,)), ((), ())), preferred_element_type=F32)


def _sigmoid(x):
    return 1.0 / (1.0 + jnp.exp(-x))


def _silu(x):
    return x * _sigmoid(x)


def _split2(x):
    hi = x.astype(BF16)
    lo = (x - hi.astype(F32)).astype(BF16)
    return hi, lo


def _split3(x):
    hi = x.astype(BF16)
    r = x - hi.astype(F32)
    mid = r.astype(BF16)
    lo = (r - mid.astype(F32)).astype(BF16)
    return hi, mid, lo


def _ada_kernel(c_ref, w_ref, b_ref, o_ref):
    o_ref[...] = _dot(_silu(c_ref[...]).astype(BF16), w_ref[...].astype(BF16)) + b_ref[...]


def _ada_call(c_pad, w_ada, b_ada):
    rows, d = c_pad.shape
    n = w_ada.shape[1]
    tn = 1024
    return pl.pallas_call(
        _ada_kernel,
        grid=(n // tn,),
        in_specs=[
            pl.BlockSpec((rows, d), lambda j: (0, 0)),
            pl.BlockSpec((d, tn), lambda j: (0, j)),
            pl.BlockSpec((1, tn), lambda j: (0, j)),
        ],
        out_specs=pl.BlockSpec((rows, tn), lambda j: (0, j)),
        out_shape=jax.ShapeDtypeStruct((rows, n), F32),
        compiler_params=_cparams(("arbitrary",)),
        name="ada_mod",
    )(c_pad, w_ada, b_ada.reshape(1, n))


def _inproj_kernel(x_ref, pos_ref, mod_ref, nm_ref, freq_ref, sign_ref, dtb_ref, bg_ref,
                   wq_ref, wk_ref, wv_ref, wz_ref, wx_ref, wg_ref, wdt_ref,
                   qt_ref, k_ref, vt_ref, z_ref, xbc_ref, gates_ref, dt_ref, *, q_scale):
    x = x_ref[0]
    ms = jnp.mean(x * x, axis=-1, keepdims=True)
    shift = mod_ref[0, 0:1, :]
    scale = mod_ref[0, 1:2, :]
    h = (x * lax.rsqrt(ms + NORM_EPS) * nm_ref[...]) * (1.0 + scale) + shift
    hb = h.astype(BF16)

    ang = pos_ref[0].astype(F32) * freq_ref[...]
    cos = jnp.cos(ang)
    sin_signed = jnp.sin(ang) * sign_ref[...]
    lane = lax.broadcasted_iota(I32, ang.shape, 1) % ATTN_QK_DIM
    first_half = lane < (ROPE_DIM // 2)

    def rope(t):
        outs = []
        for hd in range(t.shape[1] // LANES):
            th = t[:, hd * LANES:(hd + 1) * LANES]
            up = pltpu.roll(th, LANES - ROPE_DIM // 2, axis=1)
            down = pltpu.roll(th, ROPE_DIM // 2, axis=1)
            partner = jnp.where(first_half, up, down)
            outs.append(th * cos + partner * sin_signed)
        return jnp.concatenate(outs, axis=1)

    q = rope(_dot(hb, wq_ref[...])) * q_scale
    qt_ref[0, 0] = q.T.astype(BF16)
    k = rope(_dot(hb, wk_ref[...]))
    k_ref[0] = k.astype(BF16)
    v = _dot(hb, wv_ref[...])
    vt_ref[0, 0] = v.T.astype(BF16)
    z_ref[0] = _dot(hb, wz_ref[...])
    xbc_ref[0] = _dot(hb, wx_ref[...])
    g = _dot(hb, wg_ref[...]) + bg_ref[...]
    gates_ref[0] = _sigmoid(g)
    dtr = _dot(hb, wdt_ref[...]) + dtb_ref[...]
    dt_ref[0] = jnp.maximum(dtr, 0.0) + jnp.log1p(jnp.exp(-jnp.abs(dtr)))


def _inproj_call(x, positions, mod3, norm_mix, b_gate, dt_bias, w_in, tm):
    bsz, seq, d = x.shape
    dq = ATTN_HEADS * 2 * ATTN_QK_DIM
    dv = ATTN_HEADS * ATTN_V_DIM
    n_heads_ssd = dt_bias.shape[0]
    d_inner = n_heads_ssd * SSD_HEAD_DIM
    d_conv = d_inner + 2 * SSD_GROUPS * SSD_STATE
    sizes = (dq, dq, dv, d_inner, d_conv, n_heads_ssd, 2 * d)
    offs = [0]
    for s in sizes:
        offs.append(offs[-1] + s)
    wb = w_in.astype(BF16)
    wq, wk, wv, wz, wx, wdt, wg = (wb[:, offs[i]:offs[i + 1]] for i in range(7))
    wdt = jnp.pad(wdt, ((0, 0), (0, LANES - n_heads_ssd)))
    dtb = jnp.pad(dt_bias.astype(F32), (0, LANES - n_heads_ssd)).reshape(1, LANES)

    half = ROPE_DIM // 2
    lane_d = jnp.arange(LANES) % ATTN_QK_DIM
    inv_freq = 1.0 / (ROPE_THETA ** (jnp.arange(0, ROPE_DIM, 2, dtype=F32) / ROPE_DIM))
    freq = jnp.where(lane_d < ROPE_DIM, inv_freq[lane_d % half], 0.0).reshape(1, LANES).astype(F32)
    sign = jnp.where(lane_d < half, -1.0, jnp.where(lane_d < ROPE_DIM, 1.0, 0.0)).reshape(1, LANES).astype(F32)

    const2 = lambda b, i: (0, 0)
    wspec = lambda w: pl.BlockSpec(w.shape, const2, pipeline_mode=pl.Buffered(1))
    row_spec = lambda n: pl.BlockSpec((1, tm, n), lambda b, i: (b, i, 0))
    col_spec = lambda n: pl.BlockSpec((1, 1, n, tm), lambda b, i: (b, i, 0, 0))
    q_scale = (ATTN_QK_DIM ** -0.5) * LOG2E
    return pl.pallas_call(
        functools.partial(_inproj_kernel, q_scale=q_scale),
        grid=(bsz, seq // tm),
        in_specs=[
            row_spec(d),
            pl.BlockSpec((1, tm, 1), lambda b, i: (b, i, 0)),
            pl.BlockSpec((1, 6, d), lambda b, i: (b, 0, 0)),
            pl.BlockSpec((1, d), const2),
            pl.BlockSpec((1, LANES), const2),
            pl.BlockSpec((1, LANES), const2),
            pl.BlockSpec((1, LANES), const2),
            pl.BlockSpec((1, 2 * d), const2),
            wspec(wq), wspec(wk), wspec(wv), wspec(wz), wspec(wx), wspec(wg), wspec(wdt),
        ],
        out_specs=[col_spec(dq), row_spec(dq), col_spec(dv), row_spec(d_inner), row_spec(d_conv),
                   row_spec(2 * d), row_spec(LANES)],
        out_shape=[
            jax.ShapeDtypeStruct((bsz, seq // tm, dq, tm), BF16),
            jax.ShapeDtypeStruct((bsz, seq, dq), BF16),
            jax.ShapeDtypeStruct((bsz, seq // tm, dv, tm), BF16),
            jax.ShapeDtypeStruct((bsz, seq, d_inner), F32),
            jax.ShapeDtypeStruct((bsz, seq, d_conv), F32),
            jax.ShapeDtypeStruct((bsz, seq, 2 * d), F32),
            jax.ShapeDtypeStruct((bsz, seq, LANES), F32),
        ],
        compiler_params=_cparams(("arbitrary", "arbitrary")),
        name="in_proj",
    )(x, positions.reshape(bsz, seq, 1), mod3, norm_mix.reshape(1, d), freq, sign, dtb,
      b_gate.reshape(1, 2 * d), wq, wk, wv, wz, wx, wg, wdt)


def _attn_kernel(qt_ref, k_ref, vt_ref, lam_ref, gain_ref, o_ref, acc_ref, *, tq, lambda_init):
    i = pl.program_id(2)
    qt = qt_ref[0, 0]
    row = lax.broadcasted_iota(I32, qt.shape, 0)
    zero = jnp.zeros_like(qt)
    qm = jnp.concatenate([jnp.where(row < ATTN_QK_DIM, qt, zero), jnp.where(row >= ATTN_QK_DIM, qt, zero)], axis=1)
    acc_ref[...] = jnp.zeros_like(acc_ref)

    def step(j, carry, masked):
        m, l = carry
        kb = k_ref[0, pl.ds(pl.multiple_of(j * tq, tq), tq), :]
        s = _dot(kb, qm)
        if masked:
            kidx = lax.broadcasted_iota(I32, s.shape, 0)
            qidx = lax.broadcasted_iota(I32, s.shape, 1) % tq
            s = jnp.where(kidx <= qidx, s, NEG_BIG)
        m_new = jnp.maximum(m, jnp.max(s, axis=0, keepdims=True))
        alpha = jnp.exp2(m - m_new)
        p = jnp.exp2(s - m_new)
        l_new = alpha * l + jnp.sum(p, axis=0, keepdims=True)
        pv = _dot(vt_ref[0, j], p.astype(BF16))
        acc_ref[...] = acc_ref[...] * alpha + pv
        return m_new, l_new

    init = (jnp.full((1, 2 * tq), NEG_BIG, F32), jnp.zeros((1, 2 * tq), F32))
    carry = lax.fori_loop(0, i, functools.partial(step, masked=False), init)
    _, l = step(i, carry, True)

    lv = lam_ref[...]
    lam = (jnp.exp(jnp.sum(lv[0:1] * lv[1:2], axis=1, keepdims=True))
           - jnp.exp(jnp.sum(lv[2:3] * lv[3:4], axis=1, keepdims=True)) + lambda_init)
    acc = acc_ref[...]
    o = acc[:, :tq] / l[:, :tq] - lam * (acc[:, tq:] / l[:, tq:])
    ms = jnp.mean(o * o, axis=0, keepdims=True)
    o = o * lax.rsqrt(ms + NORM_EPS) * gain_ref[...] * (1.0 - lambda_init)
    o_ref[0] = o.T.astype(o_ref.dtype)


def _attn_call(qt4, k, vt4, lam_vecs, head_gain, lambda_init):
    bsz, nq, dq, tq = qt4.shape
    seq = k.shape[1]
    dv = vt4.shape[2]
    return pl.pallas_call(
        functools.partial(_attn_kernel, tq=tq, lambda_init=lambda_init),
        grid=(bsz, ATTN_HEADS, nq),
        in_specs=[
            pl.BlockSpec((1, 1, LANES, tq), lambda b, h, i: (b, i, h, 0)),
            pl.BlockSpec((1, seq, LANES), lambda b, h, i: (b, 0, h)),
            pl.BlockSpec((1, nq, ATTN_V_DIM, tq), lambda b, h, i: (b, 0, h, 0)),
            pl.BlockSpec((4, ATTN_QK_DIM), lambda b, h, i: (0, 0)),
            pl.BlockSpec((ATTN_V_DIM, 1), lambda b, h, i: (0, 0)),
        ],
        out_specs=pl.BlockSpec((1, tq, ATTN_V_DIM), lambda b, h, i: (b, i, h)),
        out_shape=jax.ShapeDtypeStruct((bsz, seq, dv), BF16),
        scratch_shapes=[pltpu.VMEM((ATTN_V_DIM, 2 * tq), F32)],
        compiler_params=_cparams(("arbitrary", "arbitrary", "arbitrary")),
        name="diff_attn",
    )(qt4, k, vt4, lam_vecs, head_gain.reshape(ATTN_V_DIM, 1))


def _ssd_kernel(xbc_ref, dt_ref, z_ref, cw_ref, cb_ref, alog_ref, dskip_ref, nw_ref, expand_ref,
                y_ref, tail_ref, state_ref, *, n_heads):
    c = pl.program_id(1)
    L = xbc_ref.shape[1]
    d_inner = n_heads * SSD_HEAD_DIM
    gw = d_inner // SSD_GROUPS
    hpg = n_heads // SSD_GROUPS

    @pl.when(c == 0)
    def _():
        tail_ref[...] = jnp.zeros_like(tail_ref)
        state_ref[...] = jnp.zeros_like(state_ref)

    u = xbc_ref[0]
    tail = tail_ref[...]
    row8 = lax.broadcasted_iota(I32, tail.shape, 0)
    conv = u * cw_ref[SSD_CONV - 1:SSD_CONV, :] + cb_ref[...]
    for s in range(1, SSD_CONV):
        r = pltpu.roll(u, s, axis=0)
        top = jnp.where(row8 < s, pltpu.roll(tail, s, axis=0), r[:SUBLANES])
        shifted = jnp.concatenate([top, r[SUBLANES:]], axis=0)
        conv = conv + shifted * cw_ref[SSD_CONV - 1 - s:SSD_CONV - s, :]
    tail_ref[...] = u[L - SUBLANES:, :]
    xbc = _silu(conv)
    xs = xbc[:, :d_inner]
    bmat = xbc[:, d_inner:d_inner + SSD_GROUPS * SSD_STATE]
    cmat = xbc[:, d_inner + SSD_GROUPS * SSD_STATE:]

    dt = dt_ref[0]
    a = dt * (-jnp.exp(alog_ref[...]))
    ri = lax.broadcasted_iota(I32, (L, L), 0)
    ci = lax.broadcasted_iota(I32, (L, L), 1)
    causal = ri >= ci
    tril = jnp.where(causal, 1.0, 0.0).astype(BF16)
    a_hi, a_mid, a_lo = _split3(a)
    a_cum = _dot(tril, a_hi) + _dot(tril, a_mid) + _dot(tril, a_lo)
    a_cum_t = a_cum.T
    a_last = a_cum[L - 1:L, :]
    ea = jnp.exp(a_cum)
    to_end = jnp.exp(a_last - a_cum)

    expand = expand_ref[...]

    def widen(v):
        hi, lo = _split2(v)
        return _dot(hi, expand) + _dot(lo, expand)

    dt_w = widen(dt)
    ea_w = widen(ea)
    te_w = widen(to_end)
    xdt = xs * dt_w
    xdt_b = xdt.astype(BF16)
    xte_b = (xdt * te_w).astype(BF16)

    y_parts = []
    for g in range(SSD_GROUPS):
        cg = cmat[:, g * SSD_STATE:(g + 1) * SSD_STATE]
        bg = bmat[:, g * SSD_STATE:(g + 1) * SSD_STATE]
        cg_b = cg.astype(BF16)
        cb = _dot_nt(cg_b, bg.astype(BF16))
        st = state_ref[g]
        y_off = _dot(cg_b, st.astype(BF16)) * ea_w[:, g * gw:(g + 1) * gw]
        y_diag = []
        for e in range(hpg):
            hh = g * hpg + e
            seg = a_cum[:, hh:hh + 1] - a_cum_t[hh:hh + 1, :]
            decay = jnp.exp(jnp.where(causal, seg, -jnp.inf))
            m = (cb * decay).astype(BF16)
            y_diag.append(_dot(m, xdt_b[:, hh * SSD_HEAD_DIM:(hh + 1) * SSD_HEAD_DIM]))
        y_parts.append(jnp.concatenate(y_diag, axis=1) + y_off)
        bt = bg.T.astype(BF16)
        state_ref[g] = st * ea_w[L - 1:L, g * gw:(g + 1) * gw] + _dot(bt, xte_b[:, g * gw:(g + 1) * gw])

    y = jnp.concatenate(y_parts, axis=1) + dskip_ref[...] * xs
    y = y * _silu(z_ref[0])
    outs = []
    for g in range(SSD_GROUPS):
        yg = y[:, g * gw:(g + 1) * gw]
        ms = jnp.mean(yg * yg, axis=-1, keepdims=True)
        outs.append(yg * lax.rsqrt(ms + NORM_EPS) * nw_ref[:, g * gw:(g + 1) * gw])
    y_ref[0] = jnp.concatenate(outs, axis=1).astype(y_ref.dtype)


def _ssd_call(xbc, dt, z, conv_w, conv_b, a_log, d_skip, ssd_norm):
    bsz, seq, d_conv = xbc.shape
    n_heads = a_log.shape[0]
    d_inner = n_heads * SSD_HEAD_DIM
    L = SSD_CHUNK
    alog = jnp.pad(a_log.astype(F32), (0, LANES - n_heads)).reshape(1, LANES)
    dskip = jnp.repeat(d_skip.astype(F32), SSD_HEAD_DIM).reshape(1, d_inner)
    head_of_lane = jnp.arange(d_inner) // SSD_HEAD_DIM
    expand = (jnp.arange(LANES)[:, None] == head_of_lane[None, :]).astype(BF16)
    const2 = lambda b, c: (0, 0)
    return pl.pallas_call(
        functools.partial(_ssd_kernel, n_heads=n_heads),
        grid=(bsz, seq // L),
        in_specs=[
            pl.BlockSpec((1, L, d_conv), lambda b, c: (b, c, 0)),
            pl.BlockSpec((1, L, LANES), lambda b, c: (b, c, 0)),
            pl.BlockSpec((1, L, d_inner), lambda b, c: (b, c, 0)),
            pl.BlockSpec((SSD_CONV, d_conv), const2),
            pl.BlockSpec((1, d_conv), const2),
            pl.BlockSpec((1, LANES), const2),
            pl.BlockSpec((1, d_inner), const2),
            pl.BlockSpec((1, d_inner), const2),
            pl.BlockSpec((LANES, d_inner), const2),
        ],
        out_specs=pl.BlockSpec((1, L, d_inner), lambda b, c: (b, c, 0)),
        out_shape=jax.ShapeDtypeStruct((bsz, seq, d_inner), BF16),
        scratch_shapes=[
            pltpu.VMEM((SUBLANES, d_conv), F32),
            pltpu.VMEM((SSD_GROUPS, SSD_STATE, d_inner // SSD_GROUPS), F32),
        ],
        compiler_params=_cparams(("arbitrary", "arbitrary")),
        name="ssd_scan",
    )(xbc, dt, z, conv_w, conv_b.reshape(1, d_conv), alog, dskip, ssd_norm.reshape(1, d_inner), expand)


def _merge_kernel(x_ref, ya_ref, ys_ref, gates_ref, mod_ref, nf_ref,
                  wba_ref, wbs_ref, wo_ref, wr_ref, wsg_ref, wsu_ref, wsd_ref,
                  h2_ref, base_ref, sc_ref):
    d = x_ref.shape[2]
    gate_m = mod_ref[0, 2:3, :]
    shift_f = mod_ref[0, 3:4, :]
    scale_f = mod_ref[0, 4:5, :]
    gate_f = mod_ref[0, 5:6, :]
    gates = gates_ref[0]
    mixed = gates[:, :d] * _dot(ya_ref[0], wba_ref[...]) + gates[:, d:] * _dot(ys_ref[0], wbs_ref[...])
    x1 = x_ref[0] + gate_m * _dot(mixed.astype(BF16), wo_ref[...])
    ms = jnp.mean(x1 * x1, axis=-1, keepdims=True)
    h2 = (x1 * lax.rsqrt(ms + NORM_EPS) * nf_ref[...]) * (1.0 + scale_f) + shift_f
    h2b = h2.astype(BF16)
    h2_ref[0] = h2b.astype(F32)
    sc_ref[0] = _sigmoid(_dot(h2b, wr_ref[...]))
    act = _silu(_dot(h2b, wsg_ref[...])) * _dot(h2b, wsu_ref[...])
    shared = _dot(act.astype(BF16), wsd_ref[...])
    base_ref[0] = x1 + gate_f * shared


def _merge_call(x, y_attn, y_ssd, gates, mod3, norm_ffn, w_ba, w_bs, w_out, w_router, w_sg, w_su, w_sd, tm):
    bsz, seq, d = x.shape
    ws = [w.astype(BF16) for w in (w_ba, w_bs, w_out, w_router, w_sg, w_su, w_sd)]
    const2 = lambda b, i: (0, 0)
    wspec = lambda w: pl.BlockSpec(w.shape, const2, pipeline_mode=pl.Buffered(1))
    row_spec = lambda n: pl.BlockSpec((1, tm, n), lambda b, i: (b, i, 0))
    return pl.pallas_call(
        _merge_kernel,
        grid=(bsz, seq // tm),
        in_specs=[row_spec(d), row_spec(y_attn.shape[2]), row_spec(y_ssd.shape[2]), row_spec(2 * d),
                  pl.BlockSpec((1, 6, d), lambda b, i: (b, 0, 0)), pl.BlockSpec((1, d), const2)]
                 + [wspec(w) for w in ws],
        out_specs=[row_spec(d), row_spec(d), row_spec(N_EXPERTS)],
        out_shape=[jax.ShapeDtypeStruct((bsz, seq, d), F32), jax.ShapeDtypeStruct((bsz, seq, d), F32),
                   jax.ShapeDtypeStruct((bsz, seq, N_EXPERTS), F32)],
        compiler_params=_cparams(("arbitrary", "arbitrary")),
        name="merge_ffn_pre",
    )(x, y_attn, y_ssd, gates, mod3, norm_ffn.reshape(1, d), *ws)


def _first_max(v, iota, sentinel):
    mx = jnp.max(v, axis=0, keepdims=True)
    first = jnp.min(jnp.where(v == mx, iota, sentinel), axis=0, keepdims=True)
    return mx, first


def _route_kernel(sc_ref, bias_ref, idx_ref, w_ref):
    sc = sc_ref[...].T
    tm = sc.shape[1]
    choice = sc + bias_ref[...]
    per_group = N_EXPERTS // N_EXPERT_GROUPS
    gi = lax.broadcasted_iota(I32, (per_group, tm), 0)
    gscore = []
    for g in range(N_EXPERT_GROUPS):
        cg = choice[g * per_group:(g + 1) * per_group, :]
        m1, f1 = _first_max(cg, gi, per_group)
        m2 = jnp.max(jnp.where(gi == f1, -jnp.inf, cg), axis=0, keepdims=True)
        gscore.append(m1 + m2)
    cur = jnp.concatenate(gscore, axis=0)
    giota = lax.broadcasted_iota(I32, cur.shape, 0)
    keep = jnp.zeros(cur.shape, F32)
    for _ in range(TOPK_GROUPS):
        _, f = _first_max(cur, giota, N_EXPERT_GROUPS)
        sel = giota == f
        keep = jnp.where(sel, 1.0, keep)
        cur = jnp.where(sel, -jnp.inf, cur)
    masked = jnp.concatenate(
        [jnp.where(keep[g:g + 1, :] > 0.5, choice[g * per_group:(g + 1) * per_group, :], -jnp.inf)
         for g in range(N_EXPERT_GROUPS)], axis=0)
    eiota = lax.broadcasted_iota(I32, masked.shape, 0)
    ids, ws = [], []
    for _ in range(TOP_K):
        _, f = _first_max(masked, eiota, N_EXPERTS)
        sel = eiota == f
        ids.append(f)
        ws.append(jnp.sum(jnp.where(sel, sc, 0.0), axis=0, keepdims=True))
        masked = jnp.where(sel, -jnp.inf, masked)
    idx_ref[...] = jnp.concatenate(ids, axis=0)
    w = jnp.concatenate(ws, axis=0)
    w_ref[...] = w / (jnp.sum(w, axis=0, keepdims=True) + 1e-20) * ROUTED_SCALE


def _route_call(scores, router_bias, tm):
    n = scores.shape[0]
    return pl.pallas_call(
        _route_kernel,
        grid=(n // tm,),
        in_specs=[pl.BlockSpec((tm, N_EXPERTS), lambda i: (i, 0)), pl.BlockSpec((N_EXPERTS, 1), lambda i: (0, 0))],
        out_specs=[pl.BlockSpec((TOP_K, tm), lambda i: (0, i)), pl.BlockSpec((TOP_K, tm), lambda i: (0, i))],
        out_shape=[jax.ShapeDtypeStruct((TOP_K, n), I32), jax.ShapeDtypeStruct((TOP_K, n), F32)],
        compiler_params=_cparams(("arbitrary",)),
        name="route_topk",
    )(scores, router_bias.astype(F32).reshape(N_EXPERTS, 1))


def _rank_kernel(idx_ref, pos_ref, cnt_ref, base_ref):
    i = pl.program_id(0)

    @pl.when(i == 0)
    def _():
        base_ref[...] = jnp.zeros_like(base_ref)

    idx = idx_ref[...]
    tm = idx.shape[1]
    eiota = lax.broadcasted_iota(I32, (N_EXPERTS, tm), 0)
    onehot = jnp.zeros((N_EXPERTS, tm), F32)
    for k in range(TOP_K):
        onehot = onehot + jnp.where(eiota == idx[k:k + 1, :], 1.0, 0.0)
    ri = lax.broadcasted_iota(I32, (tm, tm), 0)
    ci = lax.broadcasted_iota(I32, (tm, tm), 1)
    before = jnp.where(ri < ci, 1.0, 0.0).astype(BF16)
    prior = _dot(onehot.astype(BF16), before) + base_ref[:, 0:1]
    rows = []
    for k in range(TOP_K):
        rows.append(jnp.sum(jnp.where(eiota == idx[k:k + 1, :], prior, 0.0), axis=0, keepdims=True))
    pos_ref[...] = jnp.concatenate(rows, axis=0).astype(I32)
    base_ref[...] = base_ref[...] + jnp.sum(onehot, axis=1, keepdims=True)
    cnt_ref[...] = base_ref[...]


def _rank_call(idx, tm):
    n = idx.shape[1]
    return pl.pallas_call(
        _rank_kernel,
        grid=(n // tm,),
        in_specs=[pl.BlockSpec((TOP_K, tm), lambda i: (0, i))],
        out_specs=[pl.BlockSpec((TOP_K, tm), lambda i: (0, i)), pl.BlockSpec((N_EXPERTS, LANES), lambda i: (0, 0))],
        out_shape=[jax.ShapeDtypeStruct((TOP_K, n), I32), jax.ShapeDtypeStruct((N_EXPERTS, LANES), F32)],
        scratch_shapes=[pltpu.VMEM((N_EXPERTS, LANES), F32)],
        compiler_params=_cparams(("arbitrary",)),
        name="expert_rank",
    )(idx)


def _dispatch_kernel(pend_ref, padded_ref, nu_ref, dest_ref, h_ref, xs_ref, zero_ref, sem_z, sem_s, *, blk):
    i = pl.program_id(0)
    tm = h_ref.shape[0]
    n_blocks = xs_ref.shape[0] // blk

    def zero_copy(start):
        return pltpu.make_async_copy(zero_ref, xs_ref.at[pl.ds(pl.multiple_of(start, blk), blk), :], sem_z)

    @pl.when(i == 0)
    def _():
        zero_ref[...] = jnp.zeros_like(zero_ref)

        def start(e, _):
            @pl.when(padded_ref[e] > 0)
            def _():
                zero_copy(pend_ref[e] - blk).start()
            return 0

        def wait(e, _):
            @pl.when(padded_ref[e] > 0)
            def _():
                zero_copy(pend_ref[e] - blk).wait()
            return 0

        def start_idle(b, _):
            zero_copy(b * blk).start()
            return 0

        def wait_idle(b, _):
            zero_copy(b * blk).wait()
            return 0

        lax.fori_loop(0, N_EXPERTS, start, 0)
        lax.fori_loop(nu_ref[0], n_blocks, start_idle, 0)
        lax.fori_loop(0, N_EXPERTS, wait, 0)
        lax.fori_loop(nu_ref[0], n_blocks, wait_idle, 0)

    def row_copy(t, k):
        return pltpu.make_async_copy(h_ref.at[pl.ds(t, 1), :], xs_ref.at[pl.ds(dest_ref[k, t], 1), :], sem_s)

    def issue(t, _):
        for k in range(TOP_K):
            row_copy(t, k).start()
        return 0

    lax.fori_loop(0, tm, issue, 0)
    for k in range(TOP_K):
        pltpu.make_async_copy(h_ref, xs_ref.at[pl.ds(0, tm), :], sem_s).wait()


def _dispatch_call(pend, padded, n_used, dest, h2, n_slots, blk, tm):
    n, d = h2.shape
    return pl.pallas_call(
        functools.partial(_dispatch_kernel, blk=blk),
        grid_spec=pltpu.PrefetchScalarGridSpec(
            num_scalar_prefetch=3,
            grid=(n // tm,),
            in_specs=[
                pl.BlockSpec((TOP_K, tm), lambda i, pe, pa, nu: (0, i), memory_space=pltpu.SMEM),
                pl.BlockSpec((tm, d), lambda i, pe, pa, nu: (i, 0)),
            ],
            out_specs=pl.BlockSpec(memory_space=pl.ANY),
            scratch_shapes=[pltpu.VMEM((blk, d), F32), pltpu.SemaphoreType.DMA(()), pltpu.SemaphoreType.DMA(())],
        ),
        out_shape=jax.ShapeDtypeStruct((n_slots, d), F32),
        compiler_params=_cparams(("arbitrary",)),
        name="moe_dispatch",
    )(pend, padded, n_used, dest, h2)


def _expert_kernel(be_ref, nu_ref, xs_ref, wg_ref, wu_ref, wd_ref, y_ref, wg_b, wu_b, wd_b):
    i = pl.program_id(0)

    @pl.when(i < nu_ref[0])
    def _():
        prev = be_ref[jnp.maximum(i - 1, 0)]

        @pl.when((i == 0) | (be_ref[i] != prev))
        def _():
            wg_b[...] = wg_ref[0].astype(BF16)
            wu_b[...] = wu_ref[0].astype(BF16)
            wd_b[...] = wd_ref[0].astype(BF16)

        xb = xs_ref[...].astype(BF16)
        act = _silu(_dot(xb, wg_b[...])) * _dot(xb, wu_b[...])
        y_ref[...] = _dot(act.astype(BF16), wd_b[...])

    @pl.when(i >= nu_ref[0])
    def _():
        y_ref[...] = jnp.zeros_like(y_ref)


def _expert_call(block_e, n_used, xs, w_gate, w_up, w_down, blk):
    n_slots, d = xs.shape
    ff = w_gate.shape[2]
    n_blocks = n_slots // blk
    blk_map = lambda i, be, nu: (jnp.minimum(i, nu[0] - 1), 0)
    return pl.pallas_call(
        _expert_kernel,
        grid_spec=pltpu.PrefetchScalarGridSpec(
            num_scalar_prefetch=2,
            grid=(n_blocks,),
            in_specs=[
                pl.BlockSpec((blk, d), blk_map),
                pl.BlockSpec((1, d, ff), lambda i, be, nu: (be[i], 0, 0)),
                pl.BlockSpec((1, d, ff), lambda i, be, nu: (be[i], 0, 0)),
                pl.BlockSpec((1, ff, d), lambda i, be, nu: (be[i], 0, 0)),
            ],
            out_specs=pl.BlockSpec((blk, d), lambda i, be, nu: (i, 0)),
            scratch_shapes=[pltpu.VMEM((d, ff), BF16), pltpu.VMEM((d, ff), BF16), pltpu.VMEM((ff, d), BF16)],
        ),
        out_shape=jax.ShapeDtypeStruct((n_slots, d), F32),
        compiler_params=_cparams(("arbitrary",)),
        name="moe_experts",
    )(block_e, n_used, xs, w_gate, w_up, w_down)


def _combine_kernel(dest_ref, w_ref, base_ref, gf_ref, nfin_ref, y_ref, o_ref, buf_ref, sem, *, final_norm):
    tc = base_ref.shape[0]

    def issue(t, _):
        for k in range(TOP_K):
            pltpu.make_async_copy(y_ref.at[pl.ds(dest_ref[k, t], 1), :], buf_ref.at[k, pl.ds(t, 1), :], sem).start()
        return 0

    lax.fori_loop(0, tc, issue, 0)
    for k in range(TOP_K):
        pltpu.make_async_copy(y_ref.at[pl.ds(0, tc), :], buf_ref.at[k], sem).wait()
    routed = buf_ref[0] * w_ref[:, 0:1]
    for k in range(1, TOP_K):
        routed = routed + buf_ref[k] * w_ref[:, k:k + 1]
    x2 = base_ref[...] + gf_ref[0] * routed
    if final_norm:
        ms = jnp.mean(x2 * x2, axis=-1, keepdims=True)
        x2 = x2 * lax.rsqrt(ms + NORM_EPS) * nfin_ref[...]
    o_ref[...] = x2


def _combine_call(dest, w_tok, base, gate_f, norm_final, y, tokens_per_batch, tc, final_norm):
    n, d = base.shape
    steps_per_batch = tokens_per_batch // tc
    return pl.pallas_call(
        functools.partial(_combine_kernel, final_norm=final_norm),
        grid=(n // tc,),
        in_specs=[
            pl.BlockSpec((TOP_K, tc), lambda i: (0, i), memory_space=pltpu.SMEM),
            pl.BlockSpec((tc, TOP_K), lambda i: (i, 0)),
            pl.BlockSpec((tc, d), lambda i: (i, 0)),
            pl.BlockSpec((1, 1, d), lambda i: (i // steps_per_batch, 0, 0)),
            pl.BlockSpec((1, d), lambda i: (0, 0)),
            pl.BlockSpec(memory_space=pl.ANY),
        ],
        out_specs=pl.BlockSpec((tc, d), lambda i: (i, 0)),
        out_shape=jax.ShapeDtypeStruct((n, d), F32),
        scratch_shapes=[pltpu.VMEM((TOP_K, tc, d), F32), pltpu.SemaphoreType.DMA(())],
        compiler_params=_cparams(("arbitrary",)),
        name="moe_combine",
    )(dest, w_tok, base, gate_f, norm_final.reshape(1, d), y)


def _moe_layout(idx, pos, counts, n_pairs, blk):
    n_blocks = -(-n_pairs // blk) + N_EXPERTS
    padded = (counts + blk - 1) // blk * blk
    pend = jnp.cumsum(padded)
    pstart = pend - padded
    dest = pstart[idx] + pos
    n_used = pend[-1] // blk
    blk_start = jnp.arange(n_blocks, dtype=I32) * blk
    block_e = jnp.sum((pend[None, :] <= blk_start[:, None]).astype(I32), axis=1)
    block_e = jnp.minimum(block_e, N_EXPERTS - 1)
    last_e = block_e[jnp.maximum(n_used - 1, 0)]
    block_e = jnp.where(jnp.arange(n_blocks) < n_used, block_e, last_e)
    return dest.astype(I32), pend.astype(I32), padded.astype(I32), block_e, n_used.reshape(1).astype(I32), n_blocks


def kernel(x, c, positions, w_ada, b_ada, norm_mix, w_in, b_gate, lambda_q1, lambda_k1, lambda_q2, lambda_k2,
           attn_head_norm, conv_w, conv_b, dt_bias, a_log, d_skip, ssd_norm, w_branch_attn, w_branch_ssd, w_out,
           norm_ffn, w_router, router_bias, w_exp_gate, w_exp_up, w_exp_down, w_sh_gate, w_sh_up, w_sh_down,
           norm_final):
    bsz, seq, d = x.shape
    n = bsz * seq
    depth = w_ada.shape[0]
    tm = min(TOKEN_TILE, seq)
    c_pad = jnp.pad(c, ((0, -bsz % SUBLANES), (0, 0)))
    for l in range(depth):
        last = l == depth - 1
        mod3 = _ada_call(c_pad, w_ada[l], b_ada[l])[:bsz].reshape(bsz, 6, d)
        qt4, k, vt4, z, xbc, gates, dt = _inproj_call(x, positions, mod3, norm_mix[l], b_gate[l], dt_bias[l],
                                                      w_in[l], tm)
        lambda_init = 0.8 - 0.6 * math.exp(-0.3 * l)
        lam_vecs = jnp.stack([lambda_q1[l], lambda_k1[l], lambda_q2[l], lambda_k2[l]]).astype(F32)
        y_attn = _attn_call(qt4, k, vt4, lam_vecs, attn_head_norm[l].astype(F32), lambda_init)
        y_ssd = _ssd_call(xbc, dt, z, conv_w[l], conv_b[l], a_log[l], d_skip[l], ssd_norm[l])
        h2, base, scores = _merge_call(x, y_attn, y_ssd, gates, mod3, norm_ffn[l], w_branch_attn[l],
                                       w_branch_ssd[l], w_out[l], w_router[l], w_sh_gate[l], w_sh_up[l],
                                       w_sh_down[l], tm)
        idx, w_kt = _route_call(scores.reshape(n, N_EXPERTS), router_bias[l], tm)
        pos, cnt = _rank_call(idx, tm)
        counts = cnt[:, 0].astype(I32)
        dest, pend, padded, block_e, n_used, n_blocks = _moe_layout(idx, pos, counts, n * TOP_K, MOE_BLOCK)
        xs = _dispatch_call(pend, padded, n_used, dest, h2.reshape(n, d), n_blocks * MOE_BLOCK, MOE_BLOCK, tm)
        y = _expert_call(block_e, n_used, xs, w_exp_gate[l], w_exp_up[l], w_exp_down[l], MOE_BLOCK)
        out = _combine_call(dest, w_kt.T, base.reshape(n, d), mod3[:, 5:6, :], norm_final, y, seq,
                            min(COMBINE_TILE, seq), final_norm=last)
        x = out.reshape(bsz, seq, d)
    return x
```

```python
import functools
import math

import jax
import jax.numpy as jnp
from jax import lax
from jax.experimental import pallas as pl
from jax.experimental.pallas import tpu as pltpu

F32 = jnp.float32
BF16 = jnp.bfloat16
I32 = jnp.int32

ATTN_HEADS = 8
ATTN_QK_DIM = 64
ATTN_V_DIM = 128
ROPE_DIM = 16
ROPE_THETA = 500000.0
SSD_HEAD_DIM = 64
SSD_GROUPS = 4
SSD_STATE = 128
SSD_CONV = 4
SSD_CHUNK = 128
N_EXPERTS = 256
TOP_K = 8
N_EXPERT_GROUPS = 8
TOPK_GROUPS = 4
ROUTED_SCALE = 2.5
NORM_EPS = 1e-6
LOG2E = 1.4426950408889634
NEG_BIG = -1e30

LANES = 128
SUBLANES = 8
VMEM_LIMIT_BYTES = 56 * 1024 * 1024

TOKEN_TILE = 256
MOE_BLOCK = 256
COMBINE_TILE = 128
ATTN_HEADS_PER_STEP = 4


def _cparams(sem, vmem=VMEM_LIMIT_BYTES):
    return pltpu.CompilerParams(dimension_semantics=sem, vmem_limit_bytes=vmem)


def _dot(a, b):
    return jnp.dot(a, b, preferred_element_type=F32)


def _dot_nt(a, b):
    return lax.dot_general(a, b, (((1,), (1,)), ((), ())), preferred_element_type=F32)


def _sigmoid(x):
    return 1.0 / (1.0 + jnp.exp(-x))


def _silu(x):
    return x * _sigmoid(x)


def _split2(x):
    hi = x.astype(BF16)
    lo = (x - hi.astype(F32)).astype(BF16)
    return hi, lo


def _split3(x):
    hi = x.astype(BF16)
    r = x - hi.astype(F32)
    mid = r.astype(BF16)
    lo = (r - mid.astype(F32)).astype(BF16)
    return hi, mid, lo


def _ada_kernel(c_ref, w_ref, b_ref, o_ref):
    o_ref[...] = _dot(_silu(c_ref[...]).astype(BF16), w_ref[...].astype(BF16)) + b_ref[...]


def _ada_call(c_pad, w_ada, b_ada):
    rows, d = c_pad.shape
    n = w_ada.shape[1]
    tn = 1024
    return pl.pallas_call(
        _ada_kernel,
        grid=(n // tn,),
        in_specs=[
            pl.BlockSpec((rows, d), lambda j: (0, 0)),
            pl.BlockSpec((d, tn), lambda j: (0, j)),
            pl.BlockSpec((1, tn), lambda j: (0, j)),
        ],
        out_specs=pl.BlockSpec((rows, tn), lambda j: (0, j)),
        out_shape=jax.ShapeDtypeStruct((rows, n), F32),
        compiler_params=_cparams(("arbitrary",)),
        name="ada_mod",
    )(c_pad, w_ada, b_ada.reshape(1, n))


def _inproj_kernel(x_ref, pos_ref, mod_ref, nm_ref, freq_ref, sign_ref, dtb_ref, bg_ref,
                   wq_ref, wk_ref, wv_ref, wz_ref, wx_ref, wg_ref, wdt_ref,
                   qt_ref, k_ref, vt_ref, z_ref, xbc_ref, gates_ref, dt_ref, *, q_scale):
    x = x_ref[0]
    ms = jnp.mean(x * x, axis=-1, keepdims=True)
    shift = mod_ref[0, 0:1, :]
    scale = mod_ref[0, 1:2, :]
    h = (x * lax.rsqrt(ms + NORM_EPS) * nm_ref[...]) * (1.0 + scale) + shift
    hb = h.astype(BF16)

    ang = pos_ref[0].astype(F32) * freq_ref[...]
    cos = jnp.cos(ang)
    sin_signed = jnp.sin(ang) * sign_ref[...]
    lane = lax.broadcasted_iota(I32, ang.shape, 1) % ATTN_QK_DIM
    first_half = lane < (ROPE_DIM // 2)

    def rope(t):
        outs = []
        for hd in range(t.shape[1] // LANES):
            th = t[:, hd * LANES:(hd + 1) * LANES]
            up = pltpu.roll(th, LANES - ROPE_DIM // 2, axis=1)
            down = pltpu.roll(th, ROPE_DIM // 2, axis=1)
            partner = jnp.where(first_half, up, down)
            outs.append(th * cos + partner * sin_signed)
        return jnp.concatenate(outs, axis=1)

    q = rope(_dot(hb, wq_ref[...])) * q_scale
    qt_ref[0, 0] = q.T.astype(BF16)
    k = rope(_dot(hb, wk_ref[...]))
    k_ref[0] = k.astype(BF16)
    v = _dot(hb, wv_ref[...])
    vt_ref[0, 0] = v.T.astype(BF16)
    z_ref[0] = _dot(hb, wz_ref[...])
    xbc_ref[0] = _dot(hb, wx_ref[...])
    g = _dot(hb, wg_ref[...]) + bg_ref[...]
    gates_ref[0] = _sigmoid(g)
    dtr = _dot(hb, wdt_ref[...]) + dtb_ref[...]
    dt_ref[0] = jnp.maximum(dtr, 0.0) + jnp.log1p(jnp.exp(-jnp.abs(dtr)))


def _inproj_call(x, positions, mod3, norm_mix, b_gate, dt_bias, w_in, tm):
    bsz, seq, d = x.shape
    dq = ATTN_HEADS * 2 * ATTN_QK_DIM
    dv = ATTN_HEADS * ATTN_V_DIM
    n_heads_ssd = dt_bias.shape[0]
    d_inner = n_heads_ssd * SSD_HEAD_DIM
    d_conv = d_inner + 2 * SSD_GROUPS * SSD_STATE
    sizes = (dq, dq, dv, d_inner, d_conv, n_heads_ssd, 2 * d)
    offs = [0]
    for s in sizes:
        offs.append(offs[-1] + s)
    wb = w_in.astype(BF16)
    wq, wk, wv, wz, wx, wdt, wg = (wb[:, offs[i]:offs[i + 1]] for i in range(7))
    wdt = jnp.pad(wdt, ((0, 0), (0, LANES - n_heads_ssd)))
    dtb = jnp.pad(dt_bias.astype(F32), (0, LANES - n_heads_ssd)).reshape(1, LANES)

    half = ROPE_DIM // 2
    lane_d = jnp.arange(LANES) % ATTN_QK_DIM
    inv_freq = 1.0 / (ROPE_THETA ** (jnp.arange(0, ROPE_DIM, 2, dtype=F32) / ROPE_DIM))
    freq = jnp.where(lane_d < ROPE_DIM, inv_freq[lane_d % half], 0.0).reshape(1, LANES).astype(F32)
    sign = jnp.where(lane_d < half, -1.0, jnp.where(lane_d < ROPE_DIM, 1.0, 0.0)).reshape(1, LANES).astype(F32)

    const2 = lambda b, i: (0, 0)
    wspec = lambda w: pl.BlockSpec(w.shape, const2, pipeline_mode=pl.Buffered(1))
    row_spec = lambda n: pl.BlockSpec((1, tm, n), lambda b, i: (b, i, 0))
    col_spec = lambda n: pl.BlockSpec((1, 1, n, tm), lambda b, i: (b, i, 0, 0))
    q_scale = (ATTN_QK_DIM ** -0.5) * LOG2E
    return pl.pallas_call(
        functools.partial(_inproj_kernel, q_scale=q_scale),
        grid=(bsz, seq // tm),
        in_specs=[
            row_spec(d),
            pl.BlockSpec((1, tm, 1), lambda b, i: (b, i, 0)),
            pl.BlockSpec((1, 6, d), lambda b, i: (b, 0, 0)),
            pl.BlockSpec((1, d), const2),
            pl.BlockSpec((1, LANES), const2),
            pl.BlockSpec((1, LANES), const2),
            pl.BlockSpec((1, LANES), const2),
            pl.BlockSpec((1, 2 * d), const2),
            wspec(wq), wspec(wk), wspec(wv), wspec(wz), wspec(wx), wspec(wg), wspec(wdt),
        ],
        out_specs=[col_spec(dq), row_spec(dq), col_spec(dv), row_spec(d_inner), row_spec(d_conv),
                   row_spec(2 * d), row_spec(LANES)],
        out_shape=[
            jax.ShapeDtypeStruct((bsz, seq // tm, dq, tm), BF16),
            jax.ShapeDtypeStruct((bsz, seq, dq), BF16),
            jax.ShapeDtypeStruct((bsz, seq // tm, dv, tm), BF16),
            jax.ShapeDtypeStruct((bsz, seq, d_inner), F32),
            jax.ShapeDtypeStruct((bsz, seq, d_conv), F32),
            jax.ShapeDtypeStruct((bsz, seq, 2 * d), F32),
            jax.ShapeDtypeStruct((bsz, seq, LANES), F32),
        ],
        compiler_params=_cparams(("arbitrary", "arbitrary")),
        name="in_proj",
    )(x, positions.reshape(bsz, seq, 1), mod3, norm_mix.reshape(1, d), freq, sign, dtb,
      b_gate.reshape(1, 2 * d), wq, wk, wv, wz, wx, wg, wdt)


def _attn_kernel(qt_ref, k_ref, vt_ref, lam_ref, gain_ref, o_ref, acc_ref, m_ref, l_ref, qm_ref, sa_ref, sb_ref,
                 *, tq, heads, lambda_init):
    i = pl.program_id(2)
    tk = 2 * tq
    row = lax.broadcasted_iota(I32, (LANES, tq), 0)
    for h in range(heads):
        qt = qt_ref[0, 0, h * LANES:(h + 1) * LANES, :]
        zero = jnp.zeros_like(qt)
        qm_ref[h] = jnp.concatenate([jnp.where(row < ATTN_QK_DIM, qt, zero),
                                     jnp.where(row >= ATTN_QK_DIM, qt, zero)], axis=1)
    acc_ref[...] = jnp.zeros_like(acc_ref)
    m_ref[...] = jnp.full(m_ref.shape, NEG_BIG, F32)
    l_ref[...] = jnp.zeros_like(l_ref)

    def scores_into(pair, dst_ref, hs):
        koff = pl.multiple_of(pair * tk, tk)
        for h in hs:
            dst_ref[h] = _dot(k_ref[0, pl.ds(koff, tk), h * LANES:(h + 1) * LANES], qm_ref[h])

    def softmax_pv(pair, src_ref, masked, hs):
        for h in hs:
            s = src_ref[h]
            if masked:
                kidx = pair * tk + lax.broadcasted_iota(I32, s.shape, 0)
                qidx = i * tq + lax.broadcasted_iota(I32, s.shape, 1) % tq
                s = jnp.where(kidx <= qidx, s, NEG_BIG)
            m_old = m_ref[h]
            m_new = jnp.maximum(m_old, jnp.max(s, axis=0, keepdims=True))
            alpha = jnp.exp2(m_old - m_new)
            p = jnp.exp2(s - m_new)
            l_ref[h] = alpha * l_ref[h] + jnp.sum(p, axis=0, keepdims=True)
            m_ref[h] = m_new
            pb = p.astype(BF16)
            pv = (_dot(vt_ref[0, 2 * pair, h * LANES:(h + 1) * LANES, :], pb[:tq])
                  + _dot(vt_ref[0, 2 * pair + 1, h * LANES:(h + 1) * LANES, :], pb[tq:]))
            acc_ref[h] = acc_ref[h] * alpha + pv

    bufs = (sa_ref, sb_ref)
    n_full = i // 2
    all_heads = tuple(range(heads))
    scores_into(0, sa_ref, all_heads)

    def body(n, carry):
        for par in range(2):
            @pl.when(n % 2 == par)
            def _():
                for h in all_heads:
                    scores_into(n + 1, bufs[1 - par], (h,))
                    softmax_pv(n, bufs[par], False, (h,))
        return carry

    lax.fori_loop(0, n_full, body, 0)
    for par in range(2):
        @pl.when(n_full % 2 == par)
        def _():
            softmax_pv(n_full, bufs[par], True, all_heads)

    lv = lam_ref[...]
    lam = (jnp.exp(jnp.sum(lv[0:1] * lv[1:2], axis=1, keepdims=True))
           - jnp.exp(jnp.sum(lv[2:3] * lv[3:4], axis=1, keepdims=True)) + lambda_init)
    for h in range(heads):
        acc = acc_ref[h]
        l = l_ref[h]
        o = acc[:, :tq] / l[:, :tq] - lam * (acc[:, tq:] / l[:, tq:])
        ms = jnp.mean(o * o, axis=0, keepdims=True)
        o = o * lax.rsqrt(ms + NORM_EPS) * gain_ref[...] * (1.0 - lambda_init)
        o_ref[0, :, h * LANES:(h + 1) * LANES] = o.T.astype(o_ref.dtype)


def _attn_call(qt4, k, vt4, lam_vecs, head_gain, lambda_init, heads):
    bsz, nq, dq, tq = qt4.shape
    seq = k.shape[1]
    dv = vt4.shape[2]
    w = heads * LANES
    assert nq % 2 == 0, "kv blocks are consumed in pairs"
    return pl.pallas_call(
        functools.partial(_attn_kernel, tq=tq, heads=heads, lambda_init=lambda_init),
        grid=(bsz, ATTN_HEADS // heads, nq),
        in_specs=[
            pl.BlockSpec((1, 1, w, tq), lambda b, h, i: (b, i, h, 0)),
            pl.BlockSpec((1, seq, w), lambda b, h, i: (b, 0, h)),
            pl.BlockSpec((1, nq, w, tq), lambda b, h, i: (b, 0, h, 0)),
            pl.BlockSpec((4, ATTN_QK_DIM), lambda b, h, i: (0, 0)),
            pl.BlockSpec((ATTN_V_DIM, 1), lambda b, h, i: (0, 0)),
        ],
        out_specs=pl.BlockSpec((1, tq, w), lambda b, h, i: (b, i, h)),
        out_shape=jax.ShapeDtypeStruct((bsz, seq, dv), BF16),
        scratch_shapes=[pltpu.VMEM((heads, ATTN_V_DIM, 2 * tq), F32), pltpu.VMEM((heads, 1, 2 * tq), F32),
                        pltpu.VMEM((heads, 1, 2 * tq), F32), pltpu.VMEM((heads, LANES, 2 * tq), BF16),
                        pltpu.VMEM((heads, 2 * tq, 2 * tq), F32), pltpu.VMEM((heads, 2 * tq, 2 * tq), F32)],
        compiler_params=_cparams(("arbitrary", "arbitrary", "arbitrary")),
        name="diff_attn",
    )(qt4, k, vt4, lam_vecs, head_gain.reshape(ATTN_V_DIM, 1))


def _ssd_kernel(xbc_ref, dt_ref, z_ref, cw_ref, cb_ref, alog_ref, dskip_ref, nw_ref, expand_ref,
                y_ref, tail_ref, state_ref, *, n_heads):
    c = pl.program_id(1)
    L = xbc_ref.shape[1]
    d_inner = n_heads * SSD_HEAD_DIM
    gw = d_inner // SSD_GROUPS
    hpg = n_heads // SSD_GROUPS

    @pl.when(c == 0)
    def _():
        tail_ref[...] = jnp.zeros_like(tail_ref)
        state_ref[...] = jnp.zeros_like(state_ref)

    u = xbc_ref[0]
    tail = tail_ref[...]
    row8 = lax.broadcasted_iota(I32, tail.shape, 0)
    conv = u * cw_ref[SSD_CONV - 1:SSD_CONV, :] + cb_ref[...]
    for s in range(1, SSD_CONV):
        r = pltpu.roll(u, s, axis=0)
        top = jnp.where(row8 < s, pltpu.roll(tail, s, axis=0), r[:SUBLANES])
        shifted = jnp.concatenate([top, r[SUBLANES:]], axis=0)
        conv = conv + shifted * cw_ref[SSD_CONV - 1 - s:SSD_CONV - s, :]
    tail_ref[...] = u[L - SUBLANES:, :]
    xbc = _silu(conv)
    xs = xbc[:, :d_inner]
    bmat = xbc[:, d_inner:d_inner + SSD_GROUPS * SSD_STATE]
    cmat = xbc[:, d_inner + SSD_GROUPS * SSD_STATE:]

    dt = dt_ref[0]
    a = dt * (-jnp.exp(alog_ref[...]))
    ri = lax.broadcasted_iota(I32, (L, L), 0)
    ci = lax.broadcasted_iota(I32, (L, L), 1)
    causal = ri >= ci
    tril = jnp.where(causal, 1.0, 0.0).astype(BF16)
    a_hi, a_mid, a_lo = _split3(a)
    a_cum = _dot(tril, a_hi) + _dot(tril, a_mid) + _dot(tril, a_lo)
    a_cum_t = a_cum.T
    a_last = a_cum[L - 1:L, :]
    ea = jnp.exp(a_cum)
    to_end = jnp.exp(a_last - a_cum)

    expand = expand_ref[...]

    def widen(v):
        hi, lo = _split2(v)
        return _dot(hi, expand) + _dot(lo, expand)

    dt_w = widen(dt)
    ea_w = widen(ea)
    te_w = widen(to_end)
    xdt = xs * dt_w
    xdt_b = xdt.astype(BF16)
    xte_b = (xdt * te_w).astype(BF16)

    y_parts = []
    for g in range(SSD_GROUPS):
        cg = cmat[:, g * SSD_STATE:(g + 1) * SSD_STATE]
        bg = bmat[:, g * SSD_STATE:(g + 1) * SSD_STATE]
        cg_b = cg.astype(BF16)
        cb = _dot_nt(cg_b, bg.astype(BF16))
        st = state_ref[g]
        y_off = _dot(cg_b, st.astype(BF16)) * ea_w[:, g * gw:(g + 1) * gw]
        y_diag = []
        for e in range(hpg):
            hh = g * hpg + e
            seg = a_cum[:, hh:hh + 1] - a_cum_t[hh:hh + 1, :]
            decay = jnp.exp(jnp.where(causal, seg, -jnp.inf))
            m = (cb * decay).astype(BF16)
            y_diag.append(_dot(m, xdt_b[:, hh * SSD_HEAD_DIM:(hh + 1) * SSD_HEAD_DIM]))
        y_parts.append(jnp.concatenate(y_diag, axis=1) + y_off)
        bt = bg.T.astype(BF16)
        state_ref[g] = st * ea_w[L - 1:L, g * gw:(g + 1) * gw] + _dot(bt, xte_b[:, g * gw:(g + 1) * gw])

    y = jnp.concatenate(y_parts, axis=1) + dskip_ref[...] * xs
    y = y * _silu(z_ref[0])
    outs = []
    for g in range(SSD_GROUPS):
        yg = y[:, g * gw:(g + 1) * gw]
        ms = jnp.mean(yg * yg, axis=-1, keepdims=True)
        outs.append(yg * lax.rsqrt(ms + NORM_EPS) * nw_ref[:, g * gw:(g + 1) * gw])
    y_ref[0] = jnp.concatenate(outs, axis=1).astype(y_ref.dtype)


def _ssd_call(xbc, dt, z, conv_w, conv_b, a_log, d_skip, ssd_norm):
    bsz, seq, d_conv = xbc.shape
    n_heads = a_log.shape[0]
    d_inner = n_heads * SSD_HEAD_DIM
    L = SSD_CHUNK
    alog = jnp.pad(a_log.astype(F32), (0, LANES - n_heads)).reshape(1, LANES)
    dskip = jnp.repeat(d_skip.astype(F32), SSD_HEAD_DIM).reshape(1, d_inner)
    head_of_lane = jnp.arange(d_inner) // SSD_HEAD_DIM
    expand = (jnp.arange(LANES)[:, None] == head_of_lane[None, :]).astype(BF16)
    const2 = lambda b, c: (0, 0)
    return pl.pallas_call(
        functools.partial(_ssd_kernel, n_heads=n_heads),
        grid=(bsz, seq // L),
        in_specs=[
            pl.BlockSpec((1, L, d_conv), lambda b, c: (b, c, 0)),
            pl.BlockSpec((1, L, LANES), lambda b, c: (b, c, 0)),
            pl.BlockSpec((1, L, d_inner), lambda b, c: (b, c, 0)),
            pl.BlockSpec((SSD_CONV, d_conv), const2),
            pl.BlockSpec((1, d_conv), const2),
            pl.BlockSpec((1, LANES), const2),
            pl.BlockSpec((1, d_inner), const2),
            pl.BlockSpec((1, d_inner), const2),
            pl.BlockSpec((LANES, d_inner), const2),
        ],
        out_specs=pl.BlockSpec((1, L, d_inner), lambda b, c: (b, c, 0)),
        out_shape=jax.ShapeDtypeStruct((bsz, seq, d_inner), BF16),
        scratch_shapes=[
            pltpu.VMEM((SUBLANES, d_conv), F32),
            pltpu.VMEM((SSD_GROUPS, SSD_STATE, d_inner // SSD_GROUPS), F32),
        ],
        compiler_params=_cparams(("arbitrary", "arbitrary")),
        name="ssd_scan",
    )(xbc, dt, z, conv_w, conv_b.reshape(1, d_conv), alog, dskip, ssd_norm.reshape(1, d_inner), expand)


def _merge_kernel(x_ref, ya_ref, ys_ref, gates_ref, mod_ref, nf_ref,
                  wba_ref, wbs_ref, wo_ref, wr_ref, wsg_ref, wsu_ref, wsd_ref,
                  h2_ref, base_ref, sc_ref):
    d = x_ref.shape[2]
    gate_m = mod_ref[0, 2:3, :]
    shift_f = mod_ref[0, 3:4, :]
    scale_f = mod_ref[0, 4:5, :]
    gate_f = mod_ref[0, 5:6, :]
    gates = gates_ref[0]
    mixed = gates[:, :d] * _dot(ya_ref[0], wba_ref[...]) + gates[:, d:] * _dot(ys_ref[0], wbs_ref[...])
    x1 = x_ref[0] + gate_m * _dot(mixed.astype(BF16), wo_ref[...])
    ms = jnp.mean(x1 * x1, axis=-1, keepdims=True)
    h2 = (x1 * lax.rsqrt(ms + NORM_EPS) * nf_ref[...]) * (1.0 + scale_f) + shift_f
    h2b = h2.astype(BF16)
    h2_ref[0] = h2b.astype(F32)
    sc_ref[0] = _sigmoid(_dot(h2b, wr_ref[...]))
    act = _silu(_dot(h2b, wsg_ref[...])) * _dot(h2b, wsu_ref[...])
    shared = _dot(act.astype(BF16), wsd_ref[...])
    base_ref[0] = x1 + gate_f * shared


def _merge_call(x, y_attn, y_ssd, gates, mod3, norm_ffn, w_ba, w_bs, w_out, w_router, w_sg, w_su, w_sd, tm):
    bsz, seq, d = x.shape
    ws = [w.astype(BF16) for w in (w_ba, w_bs, w_out, w_router, w_sg, w_su, w_sd)]
    const2 = lambda b, i: (0, 0)
    wspec = lambda w: pl.BlockSpec(w.shape, const2, pipeline_mode=pl.Buffered(1))
    row_spec = lambda n: pl.BlockSpec((1, tm, n), lambda b, i: (b, i, 0))
    return pl.pallas_call(
        _merge_kernel,
        grid=(bsz, seq // tm),
        in_specs=[row_spec(d), row_spec(y_attn.shape[2]), row_spec(y_ssd.shape[2]), row_spec(2 * d),
                  pl.BlockSpec((1, 6, d), lambda b, i: (b, 0, 0)), pl.BlockSpec((1, d), const2)]
                 + [wspec(w) for w in ws],
        out_specs=[row_spec(d), row_spec(d), row_spec(N_EXPERTS)],
        out_shape=[jax.ShapeDtypeStruct((bsz, seq, d), F32), jax.ShapeDtypeStruct((bsz, seq, d), F32),
                   jax.ShapeDtypeStruct((bsz, seq, N_EXPERTS), F32)],
        compiler_params=_cparams(("arbitrary", "arbitrary")),
        name="merge_ffn_pre",
    )(x, y_attn, y_ssd, gates, mod3, norm_ffn.reshape(1, d), *ws)


def _first_max(v, iota, sentinel):
    mx = jnp.max(v, axis=0, keepdims=True)
    first = jnp.min(jnp.where(v == mx, iota, sentinel), axis=0, keepdims=True)
    return mx, first


def _route_kernel(sc_ref, bias_ref, idx_ref, w_ref):
    sc = sc_ref[...].T
    tm = sc.shape[1]
    choice = sc + bias_ref[...]
    per_group = N_EXPERTS // N_EXPERT_GROUPS
    gi = lax.broadcasted_iota(I32, (per_group, tm), 0)
    gscore = []
    for g in range(N_EXPERT_GROUPS):
        cg = choice[g * per_group:(g + 1) * per_group, :]
        m1, f1 = _first_max(cg, gi, per_group)
        m2 = jnp.max(jnp.where(gi == f1, -jnp.inf, cg), axis=0, keepdims=True)
        gscore.append(m1 + m2)
    cur = jnp.concatenate(gscore, axis=0)
    giota = lax.broadcasted_iota(I32, cur.shape, 0)
    keep = jnp.zeros(cur.shape, F32)
    for _ in range(TOPK_GROUPS):
        _, f = _first_max(cur, giota, N_EXPERT_GROUPS)
        sel = giota == f
        keep = jnp.where(sel, 1.0, keep)
        cur = jnp.where(sel, -jnp.inf, cur)
    masked = jnp.concatenate(
        [jnp.where(keep[g:g + 1, :] > 0.5, choice[g * per_group:(g + 1) * per_group, :], -jnp.inf)
         for g in range(N_EXPERT_GROUPS)], axis=0)
    eiota = lax.broadcasted_iota(I32, masked.shape, 0)
    ids, ws = [], []
    for _ in range(TOP_K):
        _, f = _first_max(masked, eiota, N_EXPERTS)
        sel = eiota == f
        ids.append(f)
        ws.append(jnp.sum(jnp.where(sel, sc, 0.0), axis=0, keepdims=True))
        masked = jnp.where(sel, -jnp.inf, masked)
    idx_ref[...] = jnp.concatenate(ids, axis=0)
    w = jnp.concatenate(ws, axis=0)
    w_ref[...] = w / (jnp.sum(w, axis=0, keepdims=True) + 1e-20) * ROUTED_SCALE


def _route_call(scores, router_bias, tm):
    n = scores.shape[0]
    return pl.pallas_call(
        _route_kernel,
        grid=(n // tm,),
        in_specs=[pl.BlockSpec((tm, N_EXPERTS), lambda i: (i, 0)), pl.BlockSpec((N_EXPERTS, 1), lambda i: (0, 0))],
        out_specs=[pl.BlockSpec((TOP_K, tm), lambda i: (0, i)), pl.BlockSpec((TOP_K, tm), lambda i: (0, i))],
        out_shape=[jax.ShapeDtypeStruct((TOP_K, n), I32), jax.ShapeDtypeStruct((TOP_K, n), F32)],
        compiler_params=_cparams(("arbitrary",)),
        name="route_topk",
    )(scores, router_bias.astype(F32).reshape(N_EXPERTS, 1))


def _rank_kernel(idx_ref, dest_ref, cnt_ref, base_ref, *, blk):
    phase = pl.program_id(0)
    i = pl.program_id(1)
    idx = idx_ref[...]
    tm = idx.shape[1]
    eiota = lax.broadcasted_iota(I32, (N_EXPERTS, tm), 0)
    onehot = jnp.zeros((N_EXPERTS, tm), F32)
    for k in range(TOP_K):
        onehot = onehot + jnp.where(eiota == idx[k:k + 1, :], 1.0, 0.0)

    @pl.when((phase == 0) & (i == 0))
    def _():
        base_ref[...] = jnp.zeros_like(base_ref)

    @pl.when((phase == 1) & (i == 0))
    def _():
        counts = base_ref[...]
        cnt_ref[...] = counts
        padded = jnp.floor((counts + (blk - 1)) * (1.0 / blk)) * blk
        ri = lax.broadcasted_iota(I32, (N_EXPERTS, N_EXPERTS), 0)
        ci = lax.broadcasted_iota(I32, (N_EXPERTS, N_EXPERTS), 1)
        lower = jnp.where(ci < ri, 1.0, 0.0).astype(BF16)
        hi, mid, lo = _split3(padded)
        base_ref[...] = _dot(lower, hi) + _dot(lower, mid) + _dot(lower, lo)

    @pl.when(phase == 1)
    def _():
        ri = lax.broadcasted_iota(I32, (tm, tm), 0)
        ci = lax.broadcasted_iota(I32, (tm, tm), 1)
        before = jnp.where(ri < ci, 1.0, 0.0).astype(BF16)
        prior = _dot(onehot.astype(BF16), before) + base_ref[:, 0:1]
        rows = []
        for k in range(TOP_K):
            rows.append(jnp.sum(jnp.where(eiota == idx[k:k + 1, :], prior, 0.0), axis=0, keepdims=True))
        dest_ref[...] = jnp.concatenate(rows, axis=0).astype(I32)

    base_ref[...] = base_ref[...] + jnp.sum(onehot, axis=1, keepdims=True)


def _rank_call(idx, tm, blk):
    n = idx.shape[1]
    return pl.pallas_call(
        functools.partial(_rank_kernel, blk=blk),
        grid=(2, n // tm),
        in_specs=[pl.BlockSpec((TOP_K, tm), lambda p, i: (0, i))],
        out_specs=[pl.BlockSpec((TOP_K, tm), lambda p, i: (0, i * p)),
                   pl.BlockSpec((N_EXPERTS, LANES), lambda p, i: (0, 0))],
        out_shape=[jax.ShapeDtypeStruct((TOP_K, n), I32), jax.ShapeDtypeStruct((N_EXPERTS, LANES), F32)],
        scratch_shapes=[pltpu.VMEM((N_EXPERTS, LANES), F32)],
        compiler_params=_cparams(("arbitrary", "arbitrary")),
        name="expert_rank",
    )(idx)


def _dispatch_kernel(pend_ref, padded_ref, nu_ref, dest_ref, h_ref, xs_ref, zero_ref, sem_z, sem_s, *, blk):
    i = pl.program_id(0)
    tm = h_ref.shape[0]
    n_blocks = xs_ref.shape[0] // blk

    def zero_copy(start):
        return pltpu.make_async_copy(zero_ref, xs_ref.at[pl.ds(pl.multiple_of(start, blk), blk), :], sem_z)

    @pl.when(i == 0)
    def _():
        zero_ref[...] = jnp.zeros_like(zero_ref)

        def start(e, _):
            @pl.when(padded_ref[e] > 0)
            def _():
                zero_copy(pend_ref[e] - blk).start()
            return 0

        def wait(e, _):
            @pl.when(padded_ref[e] > 0)
            def _():
                zero_copy(pend_ref[e] - blk).wait()
            return 0

        def start_idle(b, _):
            zero_copy(b * blk).start()
            return 0

        def wait_idle(b, _):
            zero_copy(b * blk).wait()
            return 0

        lax.fori_loop(0, N_EXPERTS, start, 0)
        lax.fori_loop(nu_ref[0], n_blocks, start_idle, 0)
        lax.fori_loop(0, N_EXPERTS, wait, 0)
        lax.fori_loop(nu_ref[0], n_blocks, wait_idle, 0)

    def row_copy(t, k):
        return pltpu.make_async_copy(h_ref.at[pl.ds(t, 1), :], xs_ref.at[pl.ds(dest_ref[k, t], 1), :], sem_s)

    def issue(t, _):
        for k in range(TOP_K):
            row_copy(t, k).start(priority=k % 2)
        return 0

    lax.fori_loop(0, tm, issue, 0)
    for k in range(TOP_K):
        pltpu.make_async_copy(h_ref, xs_ref.at[pl.ds(0, tm), :], sem_s).wait()


def _dispatch_call(pend, padded, n_used, dest, h2, n_slots, blk, tm):
    n, d = h2.shape
    return pl.pallas_call(
        functools.partial(_dispatch_kernel, blk=blk),
        grid_spec=pltpu.PrefetchScalarGridSpec(
            num_scalar_prefetch=3,
            grid=(n // tm,),
            in_specs=[
                pl.BlockSpec((TOP_K, tm), lambda i, pe, pa, nu: (0, i), memory_space=pltpu.SMEM),
                pl.BlockSpec((tm, d), lambda i, pe, pa, nu: (i, 0)),
            ],
            out_specs=pl.BlockSpec(memory_space=pl.ANY),
            scratch_shapes=[pltpu.VMEM((blk, d), F32), pltpu.SemaphoreType.DMA(()), pltpu.SemaphoreType.DMA(())],
        ),
        out_shape=jax.ShapeDtypeStruct((n_slots, d), F32),
        compiler_params=_cparams(("arbitrary",)),
        name="moe_dispatch",
    )(pend, padded, n_used, dest, h2)


def _expert_kernel(be_ref, nu_ref, xs_ref, wg_ref, wu_ref, wd_ref, y_ref, wg_b, wu_b, wd_b):
    i = pl.program_id(0)

    @pl.when(i < nu_ref[0])
    def _():
        prev = be_ref[jnp.maximum(i - 1, 0)]

        @pl.when((i == 0) | (be_ref[i] != prev))
        def _():
            wg_b[...] = wg_ref[0].astype(BF16)
            wu_b[...] = wu_ref[0].astype(BF16)
            wd_b[...] = wd_ref[0].astype(BF16)

        xb = xs_ref[...].astype(BF16)
        act = _silu(_dot(xb, wg_b[...])) * _dot(xb, wu_b[...])
        y_ref[...] = _dot(act.astype(BF16), wd_b[...])

    @pl.when(i >= nu_ref[0])
    def _():
        y_ref[...] = jnp.zeros_like(y_ref)


def _expert_call(block_e, n_used, xs, w_gate, w_up, w_down, blk):
    n_slots, d = xs.shape
    ff = w_gate.shape[2]
    n_blocks = n_slots // blk
    blk_map = lambda i, be, nu: (jnp.minimum(i, nu[0] - 1), 0)
    return pl.pallas_call(
        _expert_kernel,
        grid_spec=pltpu.PrefetchScalarGridSpec(
            num_scalar_prefetch=2,
            grid=(n_blocks,),
            in_specs=[
                pl.BlockSpec((blk, d), blk_map),
                pl.BlockSpec((1, d, ff), lambda i, be, nu: (be[i], 0, 0)),
                pl.BlockSpec((1, d, ff), lambda i, be, nu: (be[i], 0, 0)),
                pl.BlockSpec((1, ff, d), lambda i, be, nu: (be[i], 0, 0)),
            ],
            out_specs=pl.BlockSpec((blk, d), lambda i, be, nu: (i, 0)),
            scratch_shapes=[pltpu.VMEM((d, ff), BF16), pltpu.VMEM((d, ff), BF16), pltpu.VMEM((ff, d), BF16)],
        ),
        out_shape=jax.ShapeDtypeStruct((n_slots, d), F32),
        compiler_params=_cparams(("arbitrary",)),
        name="moe_experts",
    )(block_e, n_used, xs, w_gate, w_up, w_down)


def _combine_kernel(dest_ref, w_ref, base_ref, gf_ref, nfin_ref, y_ref, o_ref, buf_ref, sem, *, final_norm):
    tc = base_ref.shape[0]

    def issue(t, _):
        for k in range(TOP_K):
            pltpu.make_async_copy(y_ref.at[pl.ds(dest_ref[k, t], 1), :], buf_ref.at[k, pl.ds(t, 1), :],
                                  sem).start(priority=k % 2)
        return 0

    lax.fori_loop(0, tc, issue, 0)
    for k in range(TOP_K):
        pltpu.make_async_copy(y_ref.at[pl.ds(0, tc), :], buf_ref.at[k], sem).wait()
    routed = buf_ref[0] * w_ref[:, 0:1]
    for k in range(1, TOP_K):
        routed = routed + buf_ref[k] * w_ref[:, k:k + 1]
    x2 = base_ref[...] + gf_ref[0] * routed
    if final_norm:
        ms = jnp.mean(x2 * x2, axis=-1, keepdims=True)
        x2 = x2 * lax.rsqrt(ms + NORM_EPS) * nfin_ref[...]
    o_ref[...] = x2


def _combine_call(dest, w_tok, base, gate_f, norm_final, y, tokens_per_batch, tc, final_norm):
    n, d = base.shape
    steps_per_batch = tokens_per_batch // tc
    return pl.pallas_call(
        functools.partial(_combine_kernel, final_norm=final_norm),
        grid=(n // tc,),
        in_specs=[
            pl.BlockSpec((TOP_K, tc), lambda i: (0, i), memory_space=pltpu.SMEM),
            pl.BlockSpec((tc, TOP_K), lambda i: (i, 0)),
            pl.BlockSpec((tc, d), lambda i: (i, 0)),
            pl.BlockSpec((1, 1, d), lambda i: (i // steps_per_batch, 0, 0)),
            pl.BlockSpec((1, d), lambda i: (0, 0)),
            pl.BlockSpec(memory_space=pl.ANY),
        ],
        out_specs=pl.BlockSpec((tc, d), lambda i: (i, 0)),
        out_shape=jax.ShapeDtypeStruct((n, d), F32),
        scratch_shapes=[pltpu.VMEM((TOP_K, tc, d), F32), pltpu.SemaphoreType.DMA(())],
        compiler_params=_cparams(("arbitrary",)),
        name="moe_combine",
    )(dest, w_tok, base, gate_f, norm_final.reshape(1, d), y)


def _moe_blocks(n_pairs, blk):
    return -(-n_pairs // blk) + N_EXPERTS


def _moe_layout(counts, n_pairs, blk):
    n_blocks = _moe_blocks(n_pairs, blk)
    padded = (counts + blk - 1) // blk * blk
    pend = jnp.cumsum(padded)
    n_used = pend[-1] // blk
    blk_start = jnp.arange(n_blocks, dtype=I32) * blk
    block_e = jnp.sum((pend[None, :] <= blk_start[:, None]).astype(I32), axis=1)
    block_e = jnp.minimum(block_e, N_EXPERTS - 1)
    last_e = block_e[jnp.maximum(n_used - 1, 0)]
    block_e = jnp.where(jnp.arange(n_blocks) < n_used, block_e, last_e)
    return pend.astype(I32), padded.astype(I32), block_e, n_used.reshape(1).astype(I32), n_blocks


def kernel(x, c, positions, w_ada, b_ada, norm_mix, w_in, b_gate, lambda_q1, lambda_k1, lambda_q2, lambda_k2,
           attn_head_norm, conv_w, conv_b, dt_bias, a_log, d_skip, ssd_norm, w_branch_attn, w_branch_ssd, w_out,
           norm_ffn, w_router, router_bias, w_exp_gate, w_exp_up, w_exp_down, w_sh_gate, w_sh_up, w_sh_down,
           norm_final):
    bsz, seq, d = x.shape
    n = bsz * seq
    depth = w_ada.shape[0]
    tm = min(TOKEN_TILE, seq)
    c_pad = jnp.pad(c, ((0, -bsz % SUBLANES), (0, 0)))
    for l in range(depth):
        last = l == depth - 1
        mod3 = _ada_call(c_pad, w_ada[l], b_ada[l])[:bsz].reshape(bsz, 6, d)
        qt4, k, vt4, z, xbc, gates, dt = _inproj_call(x, positions, mod3, norm_mix[l], b_gate[l], dt_bias[l],
                                                      w_in[l], tm)
        lambda_init = 0.8 - 0.6 * math.exp(-0.3 * l)
        lam_vecs = jnp.stack([lambda_q1[l], lambda_k1[l], lambda_q2[l], lambda_k2[l]]).astype(F32)
        y_attn = _attn_call(qt4, k, vt4, lam_vecs, attn_head_norm[l].astype(F32), lambda_init, ATTN_HEADS_PER_STEP)
        y_ssd = _ssd_call(xbc, dt, z, conv_w[l], conv_b[l], a_log[l], d_skip[l], ssd_norm[l])
        h2, base, scores = _merge_call(x, y_attn, y_ssd, gates, mod3, norm_ffn[l], w_branch_attn[l],
                                       w_branch_ssd[l], w_out[l], w_router[l], w_sh_gate[l], w_sh_up[l],
                                       w_sh_down[l], tm)
        idx, w_kt = _route_call(scores.reshape(n, N_EXPERTS), router_bias[l], tm)
        dest, cnt = _rank_call(idx, tm, MOE_BLOCK)
        pend, padded, block_e, n_used, n_blocks = _moe_layout(cnt[:, 0].astype(I32), n * TOP_K, MOE_BLOCK)
        xs = _dispatch_call(pend, padded, n_used, dest, h2.reshape(n, d), n_blocks * MOE_BLOCK, MOE_BLOCK, tm)
        y = _expert_call(block_e, n_used, xs, w_exp_gate[l], w_exp_up[l], w_exp_down[l], MOE_BLOCK)
        out = _combine_call(dest, w_kt.T, base.reshape(n, d), mod3[:, 5:6, :], norm_final, y, seq,
                            min(COMBINE_TILE, seq), final_norm=last)
        x = out.reshape(bsz, seq, d)
    return x
```

```python
import functools
import math

import jax
import jax.numpy as jnp
from jax import lax
from jax.experimental import pallas as pl
from jax.experimental.pallas import tpu as pltpu

F32 = jnp.float32
BF16 = jnp.bfloat16
I32 = jnp.int32

ATTN_HEADS = 8
ATTN_QK_DIM = 64
ATTN_V_DIM = 128
ROPE_DIM = 16
ROPE_THETA = 500000.0
SSD_HEAD_DIM = 64
SSD_GROUPS = 4
SSD_STATE = 128
SSD_CONV = 4
SSD_CHUNK = 128
N_EXPERTS = 256
TOP_K = 8
N_EXPERT_GROUPS = 8
TOPK_GROUPS = 4
ROUTED_SCALE = 2.5
NORM_EPS = 1e-6
LOG2E = 1.4426950408889634
NEG_BIG = -1e30

LANES = 128
SUBLANES = 8
VMEM_LIMIT_BYTES = 56 * 1024 * 1024

TOKEN_TILE = 256
MOE_BLOCK = 256
COMBINE_TILE = 128
ATTN_HEADS_PER_STEP = 4
ISSUE_UNROLL = 4


def _cparams(sem, vmem=VMEM_LIMIT_BYTES):
    return pltpu.CompilerParams(dimension_semantics=sem, vmem_limit_bytes=vmem)


def _dot(a, b):
    return jnp.dot(a, b, preferred_element_type=F32)


def _dot_nt(a, b):
    return lax.dot_general(a, b, (((1,), (1,)), ((), ())), preferred_element_type=F32)


def _sigmoid(x):
    return 1.0 / (1.0 + jnp.exp(-x))


def _silu(x):
    return x * _sigmoid(x)


def _to_tiles(x):
    return x.reshape(x.shape[0], x.shape[1] // LANES, LANES)


def _from_tiles(x):
    return x.reshape(x.shape[0], x.shape[1] * x.shape[2])


def _split2(x):
    hi = x.astype(BF16)
    lo = (x - hi.astype(F32)).astype(BF16)
    return hi, lo


def _split3(x):
    hi = x.astype(BF16)
    r = x - hi.astype(F32)
    mid = r.astype(BF16)
    lo = (r - mid.astype(F32)).astype(BF16)
    return hi, mid, lo


def _ada_kernel(c_ref, w_ref, b_ref, o_ref):
    o_ref[...] = _dot(_silu(c_ref[...]).astype(BF16), w_ref[...].astype(BF16)) + b_ref[...]


def _ada_call(c_pad, w_ada, b_ada):
    rows, d = c_pad.shape
    n = w_ada.shape[1]
    tn = 1024
    return pl.pallas_call(
        _ada_kernel,
        grid=(n // tn,),
        in_specs=[
            pl.BlockSpec((rows, d), lambda j: (0, 0)),
            pl.BlockSpec((d, tn), lambda j: (0, j)),
            pl.BlockSpec((1, tn), lambda j: (0, j)),
        ],
        out_specs=pl.BlockSpec((rows, tn), lambda j: (0, j)),
        out_shape=jax.ShapeDtypeStruct((rows, n), F32),
        compiler_params=_cparams(("arbitrary",)),
        name="ada_mod",
    )(c_pad, w_ada, b_ada.reshape(1, n))


def _inproj_kernel(x_ref, pos_ref, mod_ref, nm_ref, freq_ref, sign_ref, dtb_ref, bg_ref,
                   wq_ref, wk_ref, wv_ref, wz_ref, wx_ref, wg_ref, wdt_ref,
                   qt_ref, k_ref, vt_ref, z_ref, xbc_ref, gates_ref, dt_ref, *, q_scale):
    x = x_ref[0]
    ms = jnp.mean(x * x, axis=-1, keepdims=True)
    shift = mod_ref[0, 0:1, :]
    scale = mod_ref[0, 1:2, :]
    h = (x * lax.rsqrt(ms + NORM_EPS) * nm_ref[...]) * (1.0 + scale) + shift
    hb = h.astype(BF16)

    ang = pos_ref[0].astype(F32) * freq_ref[...]
    cos = jnp.cos(ang)
    sin_signed = jnp.sin(ang) * sign_ref[...]
    lane = lax.broadcasted_iota(I32, ang.shape, 1) % ATTN_QK_DIM
    first_half = lane < (ROPE_DIM // 2)

    def rope(t):
        outs = []
        for hd in range(t.shape[1] // LANES):
            th = t[:, hd * LANES:(hd + 1) * LANES]
            up = pltpu.roll(th, LANES - ROPE_DIM // 2, axis=1)
            down = pltpu.roll(th, ROPE_DIM // 2, axis=1)
            partner = jnp.where(first_half, up, down)
            outs.append(th * cos + partner * sin_signed)
        return jnp.concatenate(outs, axis=1)

    q = rope(_dot(hb, wq_ref[...])) * q_scale
    qt_ref[0, 0] = q.T.astype(BF16)
    k = rope(_dot(hb, wk_ref[...]))
    k_ref[0] = k.astype(BF16)
    v = _dot(hb, wv_ref[...])
    vt_ref[0, 0] = v.T.astype(BF16)
    z_ref[0] = _dot(hb, wz_ref[...])
    xbc_ref[0] = _dot(hb, wx_ref[...])
    g = _dot(hb, wg_ref[...]) + bg_ref[...]
    gates_ref[0] = _sigmoid(g)
    dtr = _dot(hb, wdt_ref[...]) + dtb_ref[...]
    dt_ref[0] = jnp.maximum(dtr, 0.0) + jnp.log1p(jnp.exp(-jnp.abs(dtr)))


def _inproj_call(x, positions, mod3, norm_mix, b_gate, dt_bias, w_in, tm):
    bsz, seq, d = x.shape
    dq = ATTN_HEADS * 2 * ATTN_QK_DIM
    dv = ATTN_HEADS * ATTN_V_DIM
    n_heads_ssd = dt_bias.shape[0]
    d_inner = n_heads_ssd * SSD_HEAD_DIM
    d_conv = d_inner + 2 * SSD_GROUPS * SSD_STATE
    sizes = (dq, dq, dv, d_inner, d_conv, n_heads_ssd, 2 * d)
    offs = [0]
    for s in sizes:
        offs.append(offs[-1] + s)
    wb = w_in.astype(BF16)
    wq, wk, wv, wz, wx, wdt, wg = (wb[:, offs[i]:offs[i + 1]] for i in range(7))
    wdt = jnp.pad(wdt, ((0, 0), (0, LANES - n_heads_ssd)))
    dtb = jnp.pad(dt_bias.astype(F32), (0, LANES - n_heads_ssd)).reshape(1, LANES)

    half = ROPE_DIM // 2
    lane_d = jnp.arange(LANES) % ATTN_QK_DIM
    inv_freq = 1.0 / (ROPE_THETA ** (jnp.arange(0, ROPE_DIM, 2, dtype=F32) / ROPE_DIM))
    freq = jnp.where(lane_d < ROPE_DIM, inv_freq[lane_d % half], 0.0).reshape(1, LANES).astype(F32)
    sign = jnp.where(lane_d < half, -1.0, jnp.where(lane_d < ROPE_DIM, 1.0, 0.0)).reshape(1, LANES).astype(F32)

    const2 = lambda b, i: (0, 0)
    wspec = lambda w: pl.BlockSpec(w.shape, const2, pipeline_mode=pl.Buffered(1))
    row_spec = lambda n: pl.BlockSpec((1, tm, n), lambda b, i: (b, i, 0))
    col_spec = lambda n: pl.BlockSpec((1, 1, n, tm), lambda b, i: (b, i, 0, 0))
    q_scale = (ATTN_QK_DIM ** -0.5) * LOG2E
    return pl.pallas_call(
        functools.partial(_inproj_kernel, q_scale=q_scale),
        grid=(bsz, seq // tm),
        in_specs=[
            row_spec(d),
            pl.BlockSpec((1, tm, 1), lambda b, i: (b, i, 0)),
            pl.BlockSpec((1, 6, d), lambda b, i: (b, 0, 0)),
            pl.BlockSpec((1, d), const2),
            pl.BlockSpec((1, LANES), const2),
            pl.BlockSpec((1, LANES), const2),
            pl.BlockSpec((1, LANES), const2),
            pl.BlockSpec((1, 2 * d), const2),
            wspec(wq), wspec(wk), wspec(wv), wspec(wz), wspec(wx), wspec(wg), wspec(wdt),
        ],
        out_specs=[col_spec(dq), row_spec(dq), col_spec(dv), row_spec(d_inner), row_spec(d_conv),
                   row_spec(2 * d), row_spec(LANES)],
        out_shape=[
            jax.ShapeDtypeStruct((bsz, seq // tm, dq, tm), BF16),
            jax.ShapeDtypeStruct((bsz, seq, dq), BF16),
            jax.ShapeDtypeStruct((bsz, seq // tm, dv, tm), BF16),
            jax.ShapeDtypeStruct((bsz, seq, d_inner), F32),
            jax.ShapeDtypeStruct((bsz, seq, d_conv), F32),
            jax.ShapeDtypeStruct((bsz, seq, 2 * d), F32),
            jax.ShapeDtypeStruct((bsz, seq, LANES), F32),
        ],
        compiler_params=_cparams(("arbitrary", "arbitrary")),
        name="in_proj",
    )(x, positions.reshape(bsz, seq, 1), mod3, norm_mix.reshape(1, d), freq, sign, dtb,
      b_gate.reshape(1, 2 * d), wq, wk, wv, wz, wx, wg, wdt)


def _attn_kernel(qt_ref, k_ref, vt_ref, lam_ref, gain_ref, o_ref, acc_ref, m_ref, l_ref, qm_ref, sa_ref, sb_ref,
                 *, tq, heads, lambda_init):
    i = pl.program_id(2)
    tk = 2 * tq
    row = lax.broadcasted_iota(I32, (LANES, tq), 0)
    for h in range(heads):
        qt = qt_ref[0, 0, h * LANES:(h + 1) * LANES, :]
        zero = jnp.zeros_like(qt)
        qm_ref[h] = jnp.concatenate([jnp.where(row < ATTN_QK_DIM, qt, zero),
                                     jnp.where(row >= ATTN_QK_DIM, qt, zero)], axis=1)
    acc_ref[...] = jnp.zeros_like(acc_ref)
    m_ref[...] = jnp.full(m_ref.shape, NEG_BIG, F32)
    l_ref[...] = jnp.zeros_like(l_ref)

    def scores_into(pair, dst_ref, hs):
        koff = pl.multiple_of(pair * tk, tk)
        for h in hs:
            dst_ref[h] = _dot(k_ref[0, pl.ds(koff, tk), h * LANES:(h + 1) * LANES], qm_ref[h])

    def softmax_pv(pair, src_ref, masked, hs):
        for h in hs:
            s = src_ref[h]
            if masked:
                kidx = pair * tk + lax.broadcasted_iota(I32, s.shape, 0)
                qidx = i * tq + lax.broadcasted_iota(I32, s.shape, 1) % tq
                s = jnp.where(kidx <= qidx, s, NEG_BIG)
            m_old = m_ref[h]
            m_new = jnp.maximum(m_old, jnp.max(s, axis=0, keepdims=True))
            alpha = jnp.exp2(m_old - m_new)
            p = jnp.exp2(s - m_new)
            l_ref[h] = alpha * l_ref[h] + jnp.sum(p, axis=0, keepdims=True)
            m_ref[h] = m_new
            pb = p.astype(BF16)
            pv = (_dot(vt_ref[0, 2 * pair, h * LANES:(h + 1) * LANES, :], pb[:tq])
                  + _dot(vt_ref[0, 2 * pair + 1, h * LANES:(h + 1) * LANES, :], pb[tq:]))
            acc_ref[h] = acc_ref[h] * alpha + pv

    bufs = (sa_ref, sb_ref)
    n_full = i // 2
    all_heads = tuple(range(heads))
    scores_into(0, sa_ref, all_heads)

    def body(n, carry):
        for par in range(2):
            @pl.when(n % 2 == par)
            def _():
                for h in all_heads:
                    scores_into(n + 1, bufs[1 - par], (h,))
                    softmax_pv(n, bufs[par], False, (h,))
        return carry

    lax.fori_loop(0, n_full, body, 0)
    for par in range(2):
        @pl.when(n_full % 2 == par)
        def _():
            softmax_pv(n_full, bufs[par], True, all_heads)

    lv = lam_ref[...]
    lam = (jnp.exp(jnp.sum(lv[0:1] * lv[1:2], axis=1, keepdims=True))
           - jnp.exp(jnp.sum(lv[2:3] * lv[3:4], axis=1, keepdims=True)) + lambda_init)
    for h in range(heads):
        acc = acc_ref[h]
        l = l_ref[h]
        o = acc[:, :tq] / l[:, :tq] - lam * (acc[:, tq:] / l[:, tq:])
        ms = jnp.mean(o * o, axis=0, keepdims=True)
        o = o * lax.rsqrt(ms + NORM_EPS) * gain_ref[...] * (1.0 - lambda_init)
        o_ref[0, :, h * LANES:(h + 1) * LANES] = o.T.astype(o_ref.dtype)


def _attn_call(qt4, k, vt4, lam_vecs, head_gain, lambda_init, heads):
    bsz, nq, dq, tq = qt4.shape
    seq = k.shape[1]
    dv = vt4.shape[2]
    w = heads * LANES
    assert nq % 2 == 0, "kv blocks are consumed in pairs"
    return pl.pallas_call(
        functools.partial(_attn_kernel, tq=tq, heads=heads, lambda_init=lambda_init),
        grid=(bsz, ATTN_HEADS // heads, nq),
        in_specs=[
            pl.BlockSpec((1, 1, w, tq), lambda b, h, i: (b, i, h, 0)),
            pl.BlockSpec((1, seq, w), lambda b, h, i: (b, 0, h)),
            pl.BlockSpec((1, nq, w, tq), lambda b, h, i: (b, 0, h, 0)),
            pl.BlockSpec((4, ATTN_QK_DIM), lambda b, h, i: (0, 0)),
            pl.BlockSpec((ATTN_V_DIM, 1), lambda b, h, i: (0, 0)),
        ],
        out_specs=pl.BlockSpec((1, tq, w), lambda b, h, i: (b, i, h)),
        out_shape=jax.ShapeDtypeStruct((bsz, seq, dv), BF16),
        scratch_shapes=[pltpu.VMEM((heads, ATTN_V_DIM, 2 * tq), F32), pltpu.VMEM((heads, 1, 2 * tq), F32),
                        pltpu.VMEM((heads, 1, 2 * tq), F32), pltpu.VMEM((heads, LANES, 2 * tq), BF16),
                        pltpu.VMEM((heads, 2 * tq, 2 * tq), F32), pltpu.VMEM((heads, 2 * tq, 2 * tq), F32)],
        compiler_params=_cparams(("arbitrary", "arbitrary", "arbitrary")),
        name="diff_attn",
    )(qt4, k, vt4, lam_vecs, head_gain.reshape(ATTN_V_DIM, 1))


def _ssd_kernel(xbc_ref, dt_ref, z_ref, cw_ref, cb_ref, alog_ref, dskip_ref, nw_ref, expand_ref,
                y_ref, tail_ref, state_ref, *, n_heads):
    c = pl.program_id(1)
    L = xbc_ref.shape[1]
    d_inner = n_heads * SSD_HEAD_DIM
    gw = d_inner // SSD_GROUPS
    hpg = n_heads // SSD_GROUPS

    @pl.when(c == 0)
    def _():
        tail_ref[...] = jnp.zeros_like(tail_ref)
        state_ref[...] = jnp.zeros_like(state_ref)

    u = xbc_ref[0]
    tail = tail_ref[...]
    row8 = lax.broadcasted_iota(I32, tail.shape, 0)
    conv = u * cw_ref[SSD_CONV - 1:SSD_CONV, :] + cb_ref[...]
    for s in range(1, SSD_CONV):
        r = pltpu.roll(u, s, axis=0)
        top = jnp.where(row8 < s, pltpu.roll(tail, s, axis=0), r[:SUBLANES])
        shifted = jnp.concatenate([top, r[SUBLANES:]], axis=0)
        conv = conv + shifted * cw_ref[SSD_CONV - 1 - s:SSD_CONV - s, :]
    tail_ref[...] = u[L - SUBLANES:, :]
    xbc = _silu(conv)
    xs = xbc[:, :d_inner]
    bmat = xbc[:, d_inner:d_inner + SSD_GROUPS * SSD_STATE]
    cmat = xbc[:, d_inner + SSD_GROUPS * SSD_STATE:]

    dt = dt_ref[0]
    a = dt * (-jnp.exp(alog_ref[...]))
    ri = lax.broadcasted_iota(I32, (L, L), 0)
    ci = lax.broadcasted_iota(I32, (L, L), 1)
    causal = ri >= ci
    tril = jnp.where(causal, 1.0, 0.0).astype(BF16)
    a_hi, a_mid, a_lo = _split3(a)
    a_cum = _dot(tril, a_hi) + _dot(tril, a_mid) + _dot(tril, a_lo)
    a_cum_t = a_cum.T
    a_last = a_cum[L - 1:L, :]
    ea = jnp.exp(a_cum)
    to_end = jnp.exp(a_last - a_cum)

    expand = expand_ref[...]

    def widen(v):
        hi, lo = _split2(v)
        return _dot(hi, expand) + _dot(lo, expand)

    dt_w = widen(dt)
    ea_w = widen(ea)
    te_w = widen(to_end)
    xdt = xs * dt_w
    xdt_b = xdt.astype(BF16)
    xte_b = (xdt * te_w).astype(BF16)

    y_parts = []
    for g in range(SSD_GROUPS):
        cg = cmat[:, g * SSD_STATE:(g + 1) * SSD_STATE]
        bg = bmat[:, g * SSD_STATE:(g + 1) * SSD_STATE]
        cg_b = cg.astype(BF16)
        cb = _dot_nt(cg_b, bg.astype(BF16))
        st = state_ref[g]
        y_off = _dot(cg_b, st.astype(BF16)) * ea_w[:, g * gw:(g + 1) * gw]
        y_diag = []
        for e in range(hpg):
            hh = g * hpg + e
            seg = a_cum[:, hh:hh + 1] - a_cum_t[hh:hh + 1, :]
            decay = jnp.exp(jnp.where(causal, seg, -jnp.inf))
            m = (cb * decay).astype(BF16)
            y_diag.append(_dot(m, xdt_b[:, hh * SSD_HEAD_DIM:(hh + 1) * SSD_HEAD_DIM]))
        y_parts.append(jnp.concatenate(y_diag, axis=1) + y_off)
        bt = bg.T.astype(BF16)
        state_ref[g] = st * ea_w[L - 1:L, g * gw:(g + 1) * gw] + _dot(bt, xte_b[:, g * gw:(g + 1) * gw])

    y = jnp.concatenate(y_parts, axis=1) + dskip_ref[...] * xs
    y = y * _silu(z_ref[0])
    outs = []
    for g in range(SSD_GROUPS):
        yg = y[:, g * gw:(g + 1) * gw]
        ms = jnp.mean(yg * yg, axis=-1, keepdims=True)
        outs.append(yg * lax.rsqrt(ms + NORM_EPS) * nw_ref[:, g * gw:(g + 1) * gw])
    y_ref[0] = jnp.concatenate(outs, axis=1).astype(y_ref.dtype)


def _ssd_call(xbc, dt, z, conv_w, conv_b, a_log, d_skip, ssd_norm):
    bsz, seq, d_conv = xbc.shape
    n_heads = a_log.shape[0]
    d_inner = n_heads * SSD_HEAD_DIM
    L = SSD_CHUNK
    alog = jnp.pad(a_log.astype(F32), (0, LANES - n_heads)).reshape(1, LANES)
    dskip = jnp.repeat(d_skip.astype(F32), SSD_HEAD_DIM).reshape(1, d_inner)
    head_of_lane = jnp.arange(d_inner) // SSD_HEAD_DIM
    expand = (jnp.arange(LANES)[:, None] == head_of_lane[None, :]).astype(BF16)
    const2 = lambda b, c: (0, 0)
    return pl.pallas_call(
        functools.partial(_ssd_kernel, n_heads=n_heads),
        grid=(bsz, seq // L),
        in_specs=[
            pl.BlockSpec((1, L, d_conv), lambda b, c: (b, c, 0)),
            pl.BlockSpec((1, L, LANES), lambda b, c: (b, c, 0)),
            pl.BlockSpec((1, L, d_inner), lambda b, c: (b, c, 0)),
            pl.BlockSpec((SSD_CONV, d_conv), const2),
            pl.BlockSpec((1, d_conv), const2),
            pl.BlockSpec((1, LANES), const2),
            pl.BlockSpec((1, d_inner), const2),
            pl.BlockSpec((1, d_inner), const2),
            pl.BlockSpec((LANES, d_inner), const2),
        ],
        out_specs=pl.BlockSpec((1, L, d_inner), lambda b, c: (b, c, 0)),
        out_shape=jax.ShapeDtypeStruct((bsz, seq, d_inner), BF16),
        scratch_shapes=[
            pltpu.VMEM((SUBLANES, d_conv), F32),
            pltpu.VMEM((SSD_GROUPS, SSD_STATE, d_inner // SSD_GROUPS), F32),
        ],
        compiler_params=_cparams(("arbitrary", "arbitrary")),
        name="ssd_scan",
    )(xbc, dt, z, conv_w, conv_b.reshape(1, d_conv), alog, dskip, ssd_norm.reshape(1, d_inner), expand)


def _merge_kernel(x_ref, ya_ref, ys_ref, gates_ref, mod_ref, nf_ref,
                  wba_ref, wbs_ref, wo_ref, wr_ref, wsg_ref, wsu_ref, wsd_ref,
                  h2_ref, base_ref, sc_ref):
    d = x_ref.shape[2]
    gate_m = mod_ref[0, 2:3, :]
    shift_f = mod_ref[0, 3:4, :]
    scale_f = mod_ref[0, 4:5, :]
    gate_f = mod_ref[0, 5:6, :]
    gates = gates_ref[0]
    mixed = gates[:, :d] * _dot(ya_ref[0], wba_ref[...]) + gates[:, d:] * _dot(ys_ref[0], wbs_ref[...])
    x1 = x_ref[0] + gate_m * _dot(mixed.astype(BF16), wo_ref[...])
    ms = jnp.mean(x1 * x1, axis=-1, keepdims=True)
    h2 = (x1 * lax.rsqrt(ms + NORM_EPS) * nf_ref[...]) * (1.0 + scale_f) + shift_f
    h2b = h2.astype(BF16)
    h2_ref[0] = _to_tiles(h2b.astype(F32))
    sc_ref[0] = _sigmoid(_dot(h2b, wr_ref[...]))
    act = _silu(_dot(h2b, wsg_ref[...])) * _dot(h2b, wsu_ref[...])
    shared = _dot(act.astype(BF16), wsd_ref[...])
    base_ref[0] = x1 + gate_f * shared


def _merge_call(x, y_attn, y_ssd, gates, mod3, norm_ffn, w_ba, w_bs, w_out, w_router, w_sg, w_su, w_sd, tm):
    bsz, seq, d = x.shape
    ws = [w.astype(BF16) for w in (w_ba, w_bs, w_out, w_router, w_sg, w_su, w_sd)]
    const2 = lambda b, i: (0, 0)
    wspec = lambda w: pl.BlockSpec(w.shape, const2, pipeline_mode=pl.Buffered(1))
    row_spec = lambda n: pl.BlockSpec((1, tm, n), lambda b, i: (b, i, 0))
    return pl.pallas_call(
        _merge_kernel,
        grid=(bsz, seq // tm),
        in_specs=[row_spec(d), row_spec(y_attn.shape[2]), row_spec(y_ssd.shape[2]), row_spec(2 * d),
                  pl.BlockSpec((1, 6, d), lambda b, i: (b, 0, 0)), pl.BlockSpec((1, d), const2)]
                 + [wspec(w) for w in ws],
        out_specs=[pl.BlockSpec((1, tm, d // LANES, LANES), lambda b, i: (b, i, 0, 0)), row_spec(d),
                   row_spec(N_EXPERTS)],
        out_shape=[jax.ShapeDtypeStruct((bsz, seq, d // LANES, LANES), F32),
                   jax.ShapeDtypeStruct((bsz, seq, d), F32), jax.ShapeDtypeStruct((bsz, seq, N_EXPERTS), F32)],
        compiler_params=_cparams(("arbitrary", "arbitrary")),
        name="merge_ffn_pre",
    )(x, y_attn, y_ssd, gates, mod3, norm_ffn.reshape(1, d), *ws)


def _first_max(v, iota, sentinel):
    mx = jnp.max(v, axis=0, keepdims=True)
    first = jnp.min(jnp.where(v == mx, iota, sentinel), axis=0, keepdims=True)
    return mx, first


def _route_kernel(sc_ref, bias_ref, idx_ref, w_ref):
    sc = sc_ref[...].T
    tm = sc.shape[1]
    choice = sc + bias_ref[...]
    per_group = N_EXPERTS // N_EXPERT_GROUPS
    gi = lax.broadcasted_iota(I32, (per_group, tm), 0)
    gscore = []
    for g in range(N_EXPERT_GROUPS):
        cg = choice[g * per_group:(g + 1) * per_group, :]
        m1, f1 = _first_max(cg, gi, per_group)
        m2 = jnp.max(jnp.where(gi == f1, -jnp.inf, cg), axis=0, keepdims=True)
        gscore.append(m1 + m2)
    cur = jnp.concatenate(gscore, axis=0)
    giota = lax.broadcasted_iota(I32, cur.shape, 0)
    keep = jnp.zeros(cur.shape, F32)
    for _ in range(TOPK_GROUPS):
        _, f = _first_max(cur, giota, N_EXPERT_GROUPS)
        sel = giota == f
        keep = jnp.where(sel, 1.0, keep)
        cur = jnp.where(sel, -jnp.inf, cur)
    masked = jnp.concatenate(
        [jnp.where(keep[g:g + 1, :] > 0.5, choice[g * per_group:(g + 1) * per_group, :], -jnp.inf)
         for g in range(N_EXPERT_GROUPS)], axis=0)
    eiota = lax.broadcasted_iota(I32, masked.shape, 0)
    ids, ws = [], []
    for _ in range(TOP_K):
        _, f = _first_max(masked, eiota, N_EXPERTS)
        sel = eiota == f
        ids.append(f)
        ws.append(jnp.sum(jnp.where(sel, sc, 0.0), axis=0, keepdims=True))
        masked = jnp.where(sel, -jnp.inf, masked)
    idx_ref[...] = jnp.concatenate(ids, axis=0)
    w = jnp.concatenate(ws, axis=0)
    w = w / (jnp.sum(w, axis=0, keepdims=True) + 1e-20) * ROUTED_SCALE
    cols = [jnp.broadcast_to(w[k:k + 1, :], (LANES, tm)).T for k in range(TOP_K)]
    w_ref[...] = _to_tiles(jnp.concatenate(cols, axis=1))


def _route_call(scores, router_bias, tm):
    n = scores.shape[0]
    return pl.pallas_call(
        _route_kernel,
        grid=(n // tm,),
        in_specs=[pl.BlockSpec((tm, N_EXPERTS), lambda i: (i, 0)), pl.BlockSpec((N_EXPERTS, 1), lambda i: (0, 0))],
        out_specs=[pl.BlockSpec((TOP_K, tm), lambda i: (0, i)), pl.BlockSpec((tm, TOP_K, LANES), lambda i: (i, 0, 0))],
        out_shape=[jax.ShapeDtypeStruct((TOP_K, n), I32), jax.ShapeDtypeStruct((n, TOP_K, LANES), F32)],
        compiler_params=_cparams(("arbitrary",)),
        name="route_topk",
    )(scores, router_bias.astype(F32).reshape(N_EXPERTS, 1))


def _rank_kernel(idx_ref, dest_ref, cnt_ref, base_ref, *, blk):
    phase = pl.program_id(0)
    i = pl.program_id(1)
    idx = idx_ref[...]
    tm = idx.shape[1]
    eiota = lax.broadcasted_iota(I32, (N_EXPERTS, tm), 0)
    onehot = jnp.zeros((N_EXPERTS, tm), F32)
    for k in range(TOP_K):
        onehot = onehot + jnp.where(eiota == idx[k:k + 1, :], 1.0, 0.0)

    @pl.when((phase == 0) & (i == 0))
    def _():
        base_ref[...] = jnp.zeros_like(base_ref)

    @pl.when((phase == 1) & (i == 0))
    def _():
        counts = base_ref[...]
        cnt_ref[...] = counts
        padded = jnp.floor((counts + (blk - 1)) * (1.0 / blk)) * blk
        ri = lax.broadcasted_iota(I32, (N_EXPERTS, N_EXPERTS), 0)
        ci = lax.broadcasted_iota(I32, (N_EXPERTS, N_EXPERTS), 1)
        lower = jnp.where(ci < ri, 1.0, 0.0).astype(BF16)
        hi, mid, lo = _split3(padded)
        base_ref[...] = _dot(lower, hi) + _dot(lower, mid) + _dot(lower, lo)

    @pl.when(phase == 1)
    def _():
        ri = lax.broadcasted_iota(I32, (tm, tm), 0)
        ci = lax.broadcasted_iota(I32, (tm, tm), 1)
        before = jnp.where(ri < ci, 1.0, 0.0).astype(BF16)
        prior = _dot(onehot.astype(BF16), before) + base_ref[:, 0:1]
        rows = []
        for k in range(TOP_K):
            rows.append(jnp.sum(jnp.where(eiota == idx[k:k + 1, :], prior, 0.0), axis=0, keepdims=True))
        dest_ref[...] = jnp.concatenate(rows, axis=0).astype(I32)

    base_ref[...] = base_ref[...] + jnp.sum(onehot, axis=1, keepdims=True)


def _rank_call(idx, tm, blk):
    n = idx.shape[1]
    return pl.pallas_call(
        functools.partial(_rank_kernel, blk=blk),
        grid=(2, n // tm),
        in_specs=[pl.BlockSpec((TOP_K, tm), lambda p, i: (0, i))],
        out_specs=[pl.BlockSpec((TOP_K, tm), lambda p, i: (0, i * p)),
                   pl.BlockSpec((N_EXPERTS, LANES), lambda p, i: (0, 0))],
        out_shape=[jax.ShapeDtypeStruct((TOP_K, n), I32), jax.ShapeDtypeStruct((N_EXPERTS, LANES), F32)],
        scratch_shapes=[pltpu.VMEM((N_EXPERTS, LANES), F32)],
        compiler_params=_cparams(("arbitrary", "arbitrary")),
        name="expert_rank",
    )(idx)


def _dispatch_kernel(pend_ref, padded_ref, nu_ref, dest_ref, h_ref, xs_ref, zero_ref, sem_z, sem_s, *, blk):
    i = pl.program_id(0)
    tm = h_ref.shape[0]
    n_blocks = xs_ref.shape[0] // blk

    def zero_copy(start):
        return pltpu.make_async_copy(zero_ref, xs_ref.at[pl.ds(pl.multiple_of(start, blk), blk)], sem_z)

    @pl.when(i == 0)
    def _():
        zero_ref[...] = jnp.zeros_like(zero_ref)

        def start(e, _):
            @pl.when(padded_ref[e] > 0)
            def _():
                zero_copy(pend_ref[e] - blk).start()
            return 0

        def wait(e, _):
            @pl.when(padded_ref[e] > 0)
            def _():
                zero_copy(pend_ref[e] - blk).wait()
            return 0

        def start_idle(b, _):
            zero_copy(b * blk).start()
            return 0

        def wait_idle(b, _):
            zero_copy(b * blk).wait()
            return 0

        lax.fori_loop(0, N_EXPERTS, start, 0)
        lax.fori_loop(nu_ref[0], n_blocks, start_idle, 0)
        lax.fori_loop(0, N_EXPERTS, wait, 0)
        lax.fori_loop(nu_ref[0], n_blocks, wait_idle, 0)

    def issue(g, _):
        for u in range(ISSUE_UNROLL):
            t = g * ISSUE_UNROLL + u
            for k in range(TOP_K):
                pltpu.make_async_copy(h_ref.at[t], xs_ref.at[dest_ref[k, t]], sem_s).start(priority=k % 2)
        return 0

    lax.fori_loop(0, tm // ISSUE_UNROLL, issue, 0)
    for k in range(TOP_K):
        pltpu.make_async_copy(h_ref, xs_ref.at[pl.ds(0, tm)], sem_s).wait()


def _dispatch_call(pend, padded, n_used, dest, h2t, n_slots, blk, tm):
    n, s, lanes = h2t.shape
    return pl.pallas_call(
        functools.partial(_dispatch_kernel, blk=blk),
        grid_spec=pltpu.PrefetchScalarGridSpec(
            num_scalar_prefetch=3,
            grid=(n // tm,),
            in_specs=[
                pl.BlockSpec((TOP_K, tm), lambda i, pe, pa, nu: (0, i), memory_space=pltpu.SMEM),
                pl.BlockSpec((tm, s, lanes), lambda i, pe, pa, nu: (i, 0, 0)),
            ],
            out_specs=pl.BlockSpec(memory_space=pl.ANY),
            scratch_shapes=[pltpu.VMEM((blk, s, lanes), F32), pltpu.SemaphoreType.DMA(()),
                            pltpu.SemaphoreType.DMA(())],
        ),
        out_shape=jax.ShapeDtypeStruct((n_slots, s, lanes), F32),
        compiler_params=_cparams(("arbitrary",)),
        name="moe_dispatch",
    )(pend, padded, n_used, dest, h2t)


def _expert_kernel(be_ref, nu_ref, first_ref, slot_ref, nxt_ref, xs_ref, wg_hbm, wu_hbm, wd_hbm, y_ref,
                   wg_f, wu_f, wd_f, wg_b, wu_b, wd_b, sems):
    i = pl.program_id(0)

    def weight_copies(e, slot):
        return (pltpu.make_async_copy(wg_hbm.at[e], wg_f.at[slot], sems.at[slot]),
                pltpu.make_async_copy(wu_hbm.at[e], wu_f.at[slot], sems.at[slot]),
                pltpu.make_async_copy(wd_hbm.at[e], wd_f.at[slot], sems.at[slot]))

    @pl.when(i < nu_ref[0])
    def _():
        @pl.when(first_ref[i] == 1)
        def _():
            slot = slot_ref[i]

            @pl.when(i == 0)
            def _():
                for cp in weight_copies(be_ref[0], slot):
                    cp.start()

            @pl.when(nxt_ref[i] >= 0)
            def _():
                for cp in weight_copies(nxt_ref[i], 1 - slot):
                    cp.start()

            for cp in weight_copies(be_ref[i], slot):
                cp.wait()
            wg_b[...] = wg_f[slot].astype(BF16)
            wu_b[...] = wu_f[slot].astype(BF16)
            wd_b[...] = wd_f[slot].astype(BF16)

        xb = _from_tiles(xs_ref[...]).astype(BF16)
        act = _silu(_dot(xb, wg_b[...])) * _dot(xb, wu_b[...])
        y_ref[...] = _to_tiles(_dot(act.astype(BF16), wd_b[...]))

    @pl.when(i >= nu_ref[0])
    def _():
        y_ref[...] = jnp.zeros_like(y_ref)


def _expert_call(block_e, n_used, first, slot, nxt, xs, w_gate, w_up, w_down, blk):
    n_slots, s, lanes = xs.shape
    _, d, ff = w_gate.shape
    n_blocks = n_slots // blk
    blk_map = lambda i, be, nu, fi, sl, nx: (jnp.minimum(i, nu[0] - 1), 0, 0)
    hbm = pl.BlockSpec(memory_space=pl.ANY)
    return pl.pallas_call(
        _expert_kernel,
        grid_spec=pltpu.PrefetchScalarGridSpec(
            num_scalar_prefetch=5,
            grid=(n_blocks,),
            in_specs=[pl.BlockSpec((blk, s, lanes), blk_map), hbm, hbm, hbm],
            out_specs=pl.BlockSpec((blk, s, lanes), lambda i, be, nu, fi, sl, nx: (i, 0, 0)),
            scratch_shapes=[pltpu.VMEM((2, d, ff), F32), pltpu.VMEM((2, d, ff), F32), pltpu.VMEM((2, ff, d), F32),
                            pltpu.VMEM((d, ff), BF16), pltpu.VMEM((d, ff), BF16), pltpu.VMEM((ff, d), BF16),
                            pltpu.SemaphoreType.DMA((2,))],
        ),
        out_shape=jax.ShapeDtypeStruct((n_slots, s, lanes), F32),
        compiler_params=_cparams(("arbitrary",)),
        name="moe_experts",
    )(block_e, n_used, first, slot, nxt, xs, w_gate, w_up, w_down)


def _combine_kernel(dest_ref, w_ref, base_ref, gf_ref, nfin_ref, y_ref, o_ref, buf_ref, sem, *, final_norm):
    tc = base_ref.shape[0]

    def issue(g, _):
        for u in range(ISSUE_UNROLL):
            t = g * ISSUE_UNROLL + u
            for k in range(TOP_K):
                pltpu.make_async_copy(y_ref.at[dest_ref[k, t]], buf_ref.at[k, t], sem).start(priority=k % 2)
        return 0

    lax.fori_loop(0, tc // ISSUE_UNROLL, issue, 0)
    for k in range(TOP_K):
        pltpu.make_async_copy(y_ref.at[pl.ds(0, tc)], buf_ref.at[k], sem).wait()
    routed = buf_ref[0] * w_ref[:, 0:1, :]
    for k in range(1, TOP_K):
        routed = routed + buf_ref[k] * w_ref[:, k:k + 1, :]
    x2 = base_ref[...] + gf_ref[0] * _from_tiles(routed)
    if final_norm:
        ms = jnp.mean(x2 * x2, axis=-1, keepdims=True)
        x2 = x2 * lax.rsqrt(ms + NORM_EPS) * nfin_ref[...]
    o_ref[...] = x2


def _combine_call(dest, w_tok, base, gate_f, norm_final, y, tokens_per_batch, tc, final_norm):
    n, d = base.shape
    steps_per_batch = tokens_per_batch // tc
    return pl.pallas_call(
        functools.partial(_combine_kernel, final_norm=final_norm),
        grid=(n // tc,),
        in_specs=[
            pl.BlockSpec((TOP_K, tc), lambda i: (0, i), memory_space=pltpu.SMEM),
            pl.BlockSpec((tc, TOP_K, LANES), lambda i: (i, 0, 0)),
            pl.BlockSpec((tc, d), lambda i: (i, 0)),
            pl.BlockSpec((1, 1, d), lambda i: (i // steps_per_batch, 0, 0)),
            pl.BlockSpec((1, d), lambda i: (0, 0)),
            pl.BlockSpec(memory_space=pl.ANY),
        ],
        out_specs=pl.BlockSpec((tc, d), lambda i: (i, 0)),
        out_shape=jax.ShapeDtypeStruct((n, d), F32),
        scratch_shapes=[pltpu.VMEM((TOP_K, tc, d // LANES, LANES), F32), pltpu.SemaphoreType.DMA(())],
        compiler_params=_cparams(("arbitrary",)),
        name="moe_combine",
    )(dest, w_tok, base, gate_f, norm_final.reshape(1, d), y)


def _moe_blocks(n_pairs, blk):
    return -(-n_pairs // blk) + N_EXPERTS


def _moe_layout(counts, n_pairs, blk):
    n_blocks = _moe_blocks(n_pairs, blk)
    padded = (counts + blk - 1) // blk * blk
    pend = jnp.cumsum(padded)
    n_used = pend[-1] // blk
    blk_start = jnp.arange(n_blocks, dtype=I32) * blk
    block_e = jnp.sum((pend[None, :] <= blk_start[:, None]).astype(I32), axis=1)
    block_e = jnp.minimum(block_e, N_EXPERTS - 1)
    used = jnp.arange(n_blocks) < n_used
    prev_e = jnp.concatenate([jnp.full((1,), -1, I32), block_e[:-1]])
    first = (used & (block_e != prev_e)).astype(I32)
    slot = (jnp.cumsum(first) - 1) % 2
    idx_first = jnp.where(first == 1, jnp.arange(n_blocks, dtype=I32), n_blocks)
    nxt_pos = lax.cummin(jnp.concatenate([idx_first[1:], jnp.full((1,), n_blocks, I32)]), reverse=True)
    nxt = jnp.where(nxt_pos < n_blocks, block_e[jnp.minimum(nxt_pos, n_blocks - 1)], -1)
    return (pend.astype(I32), padded.astype(I32), block_e.astype(I32), n_used.reshape(1).astype(I32),
            first, slot.astype(I32), nxt.astype(I32), n_blocks)


def kernel(x, c, positions, w_ada, b_ada, norm_mix, w_in, b_gate, lambda_q1, lambda_k1, lambda_q2, lambda_k2,
           attn_head_norm, conv_w, conv_b, dt_bias, a_log, d_skip, ssd_norm, w_branch_attn, w_branch_ssd, w_out,
           norm_ffn, w_router, router_bias, w_exp_gate, w_exp_up, w_exp_down, w_sh_gate, w_sh_up, w_sh_down,
           norm_final):
    bsz, seq, d = x.shape
    n = bsz * seq
    depth = w_ada.shape[0]
    tm = min(TOKEN_TILE, seq)
    c_pad = jnp.pad(c, ((0, -bsz % SUBLANES), (0, 0)))
    for l in range(depth):
        last = l == depth - 1
        mod3 = _ada_call(c_pad, w_ada[l], b_ada[l])[:bsz].reshape(bsz, 6, d)
        qt4, k, vt4, z, xbc, gates, dt = _inproj_call(x, positions, mod3, norm_mix[l], b_gate[l], dt_bias[l],
                                                      w_in[l], tm)
        lambda_init = 0.8 - 0.6 * math.exp(-0.3 * l)
        lam_vecs = jnp.stack([lambda_q1[l], lambda_k1[l], lambda_q2[l], lambda_k2[l]]).astype(F32)
        y_attn = _attn_call(qt4, k, vt4, lam_vecs, attn_head_norm[l].astype(F32), lambda_init, ATTN_HEADS_PER_STEP)
        y_ssd = _ssd_call(xbc, dt, z, conv_w[l], conv_b[l], a_log[l], d_skip[l], ssd_norm[l])
        h2, base, scores = _merge_call(x, y_attn, y_ssd, gates, mod3, norm_ffn[l], w_branch_attn[l],
                                       w_branch_ssd[l], w_out[l], w_router[l], w_sh_gate[l], w_sh_up[l],
                                       w_sh_down[l], tm)
        idx, w_tiles = _route_call(scores.reshape(n, N_EXPERTS), router_bias[l], tm)
        dest, cnt = _rank_call(idx, tm, MOE_BLOCK)
        pend, padded, block_e, n_used, first, slot, nxt, n_blocks = _moe_layout(cnt[:, 0].astype(I32), n * TOP_K,
                                                                                MOE_BLOCK)
        xs = _dispatch_call(pend, padded, n_used, dest, h2.reshape(n, d // LANES, LANES), n_blocks * MOE_BLOCK,
                            MOE_BLOCK, tm)
        y = _expert_call(block_e, n_used, first, slot, nxt, xs, w_exp_gate[l], w_exp_up[l], w_exp_down[l], MOE_BLOCK)
        out = _combine_call(dest, w_tiles, base.reshape(n, d), mod3[:, 5:6, :], norm_final, y, seq,
                            min(COMBINE_TILE, seq), final_norm=last)
        x = out.reshape(bsz, seq, d)
    return x
```

```python
import functools
import math

import jax
import jax.numpy as jnp
from jax import lax
from jax.experimental import pallas as pl
from jax.experimental.pallas import tpu as pltpu

F32 = jnp.float32
BF16 = jnp.bfloat16
I32 = jnp.int32

ATTN_HEADS = 8
ATTN_QK_DIM = 64
ATTN_V_DIM = 128
ROPE_DIM = 16
ROPE_THETA = 500000.0
SSD_HEAD_DIM = 64
SSD_GROUPS = 4
SSD_STATE = 128
SSD_CONV = 4
SSD_CHUNK = 128
N_EXPERTS = 256
TOP_K = 8
N_EXPERT_GROUPS = 8
TOPK_GROUPS = 4
ROUTED_SCALE = 2.5
NORM_EPS = 1e-6
LOG2E = 1.4426950408889634
NEG_BIG = -1e30

LANES = 128
SUBLANES = 8
VMEM_LIMIT_BYTES = 56 * 1024 * 1024

TOKEN_TILE = 256
MOE_BLOCK = 256
COMBINE_TILE = 128
ATTN_HEADS_PER_STEP = 4
ISSUE_UNROLL = 4


def _cparams(sem, vmem=VMEM_LIMIT_BYTES):
    return pltpu.CompilerParams(dimension_semantics=sem, vmem_limit_bytes=vmem)


def _dot(a, b):
    return jnp.dot(a, b, preferred_element_type=F32)


def _dot_nt(a, b):
    return lax.dot_general(a, b, (((1,), (1,)), ((), ())), preferred_element_type=F32)


def _sigmoid(x):
    return 1.0 / (1.0 + jnp.exp(-x))


def _silu(x):
    return x * _sigmoid(x)


def _to_tiles(x):
    return x.reshape(x.shape[0], x.shape[1] // LANES, LANES)


def _from_tiles(x):
    return x.reshape(x.shape[0], x.shape[1] * x.shape[2])


def _split2(x):
    hi = x.astype(BF16)
    lo = (x - hi.astype(F32)).astype(BF16)
    return hi, lo


def _split3(x):
    hi = x.astype(BF16)
    r = x - hi.astype(F32)
    mid = r.astype(BF16)
    lo = (r - mid.astype(F32)).astype(BF16)
    return hi, mid, lo


def _ada_kernel(c_ref, w_ref, b_ref, o_ref):
    o_ref[...] = _dot(_silu(c_ref[...]).astype(BF16), w_ref[...].astype(BF16)) + b_ref[...]


def _ada_call(c_pad, w_ada, b_ada):
    rows, d = c_pad.shape
    n = w_ada.shape[1]
    tn = 1024
    return pl.pallas_call(
        _ada_kernel,
        grid=(n // tn,),
        in_specs=[
            pl.BlockSpec((rows, d), lambda j: (0, 0)),
            pl.BlockSpec((d, tn), lambda j: (0, j)),
            pl.BlockSpec((1, tn), lambda j: (0, j)),
        ],
        out_specs=pl.BlockSpec((rows, tn), lambda j: (0, j)),
        out_shape=jax.ShapeDtypeStruct((rows, n), F32),
        compiler_params=_cparams(("arbitrary",)),
        name="ada_mod",
    )(c_pad, w_ada, b_ada.reshape(1, n))


def _inproj_kernel(x_ref, pos_ref, mod_ref, nm_ref, freq_ref, sign_ref, dtb_ref, bg_ref,
                   wq_ref, wk_ref, wv_ref, wz_ref, wx_ref, wg_ref, wdt_ref,
                   qt_ref, k_ref, vt_ref, z_ref, xbc_ref, gates_ref, dt_ref, *, q_scale):
    x = x_ref[0]
    ms = jnp.mean(x * x, axis=-1, keepdims=True)
    shift = mod_ref[0, 0:1, :]
    scale = mod_ref[0, 1:2, :]
    h = (x * lax.rsqrt(ms + NORM_EPS) * nm_ref[...]) * (1.0 + scale) + shift
    hb = h.astype(BF16)

    ang = pos_ref[0].astype(F32) * freq_ref[...]
    cos = jnp.cos(ang)
    sin_signed = jnp.sin(ang) * sign_ref[...]
    lane = lax.broadcasted_iota(I32, ang.shape, 1) % ATTN_QK_DIM
    first_half = lane < (ROPE_DIM // 2)

    def rope(t):
        outs = []
        for hd in range(t.shape[1] // LANES):
            th = t[:, hd * LANES:(hd + 1) * LANES]
            up = pltpu.roll(th, LANES - ROPE_DIM // 2, axis=1)
            down = pltpu.roll(th, ROPE_DIM // 2, axis=1)
            partner = jnp.where(first_half, up, down)
            outs.append(th * cos + partner * sin_signed)
        return jnp.concatenate(outs, axis=1)

    q = rope(_dot(hb, wq_ref[...])) * q_scale
    qt_ref[0, 0] = q.T.astype(BF16)
    k = rope(_dot(hb, wk_ref[...]))
    k_ref[0] = k.astype(BF16)
    v = _dot(hb, wv_ref[...])
    vt_ref[0, 0] = v.T.astype(BF16)
    z_ref[0] = _dot(hb, wz_ref[...])
    xbc_ref[0] = _dot(hb, wx_ref[...])
    g = _dot(hb, wg_ref[...]) + bg_ref[...]
    gates_ref[0] = _sigmoid(g)
    dtr = _dot(hb, wdt_ref[...]) + dtb_ref[...]
    dt_ref[0] = jnp.maximum(dtr, 0.0) + jnp.log1p(jnp.exp(-jnp.abs(dtr)))


def _inproj_call(x, positions, mod3, norm_mix, b_gate, dt_bias, w_in, tm):
    bsz, seq, d = x.shape
    dq = ATTN_HEADS * 2 * ATTN_QK_DIM
    dv = ATTN_HEADS * ATTN_V_DIM
    n_heads_ssd = dt_bias.shape[0]
    d_inner = n_heads_ssd * SSD_HEAD_DIM
    d_conv = d_inner + 2 * SSD_GROUPS * SSD_STATE
    sizes = (dq, dq, dv, d_inner, d_conv, n_heads_ssd, 2 * d)
    offs = [0]
    for s in sizes:
        offs.append(offs[-1] + s)
    wb = w_in.astype(BF16)
    wq, wk, wv, wz, wx, wdt, wg = (wb[:, offs[i]:offs[i + 1]] for i in range(7))
    wdt = jnp.pad(wdt, ((0, 0), (0, LANES - n_heads_ssd)))
    dtb = jnp.pad(dt_bias.astype(F32), (0, LANES - n_heads_ssd)).reshape(1, LANES)

    half = ROPE_DIM // 2
    lane_d = jnp.arange(LANES) % ATTN_QK_DIM
    inv_freq = 1.0 / (ROPE_THETA ** (jnp.arange(0, ROPE_DIM, 2, dtype=F32) / ROPE_DIM))
    freq = jnp.where(lane_d < ROPE_DIM, inv_freq[lane_d % half], 0.0).reshape(1, LANES).astype(F32)
    sign = jnp.where(lane_d < half, -1.0, jnp.where(lane_d < ROPE_DIM, 1.0, 0.0)).reshape(1, LANES).astype(F32)

    const2 = lambda b, i: (0, 0)
    wspec = lambda w: pl.BlockSpec(w.shape, const2, pipeline_mode=pl.Buffered(1))
    row_spec = lambda n: pl.BlockSpec((1, tm, n), lambda b, i: (b, i, 0))
    col_spec = lambda n: pl.BlockSpec((1, 1, n, tm), lambda b, i: (b, i, 0, 0))
    q_scale = (ATTN_QK_DIM ** -0.5) * LOG2E
    return pl.pallas_call(
        functools.partial(_inproj_kernel, q_scale=q_scale),
        grid=(bsz, seq // tm),
        in_specs=[
            row_spec(d),
            pl.BlockSpec((1, tm, 1), lambda b, i: (b, i, 0)),
            pl.BlockSpec((1, 6, d), lambda b, i: (b, 0, 0)),
            pl.BlockSpec((1, d), const2),
            pl.BlockSpec((1, LANES), const2),
            pl.BlockSpec((1, LANES), const2),
            pl.BlockSpec((1, LANES), const2),
            pl.BlockSpec((1, 2 * d), const2),
            wspec(wq), wspec(wk), wspec(wv), wspec(wz), wspec(wx), wspec(wg), wspec(wdt),
        ],
        out_specs=[col_spec(dq), row_spec(dq), col_spec(dv), row_spec(d_inner), row_spec(d_conv),
                   row_spec(2 * d), row_spec(LANES)],
        out_shape=[
            jax.ShapeDtypeStruct((bsz, seq // tm, dq, tm), BF16),
            jax.ShapeDtypeStruct((bsz, seq, dq), BF16),
            jax.ShapeDtypeStruct((bsz, seq // tm, dv, tm), BF16),
            jax.ShapeDtypeStruct((bsz, seq, d_inner), F32),
            jax.ShapeDtypeStruct((bsz, seq, d_conv), F32),
            jax.ShapeDtypeStruct((bsz, seq, 2 * d), F32),
            jax.ShapeDtypeStruct((bsz, seq, LANES), F32),
        ],
        compiler_params=_cparams(("arbitrary", "arbitrary")),
        name="in_proj",
    )(x, positions.reshape(bsz, seq, 1), mod3, norm_mix.reshape(1, d), freq, sign, dtb,
      b_gate.reshape(1, 2 * d), wq, wk, wv, wz, wx, wg, wdt)


def _attn_kernel(qt_ref, k_ref, vt_ref, lam_ref, gain_ref, o_ref, acc_ref, m_ref, l_ref, qm_ref, sa_ref, sb_ref,
                 *, tq, heads, lambda_init):
    i = pl.program_id(2)
    tk = 2 * tq
    row = lax.broadcasted_iota(I32, (LANES, tq), 0)
    for h in range(heads):
        qt = qt_ref[0, 0, h * LANES:(h + 1) * LANES, :]
        zero = jnp.zeros_like(qt)
        qm_ref[h] = jnp.concatenate([jnp.where(row < ATTN_QK_DIM, qt, zero),
                                     jnp.where(row >= ATTN_QK_DIM, qt, zero)], axis=1)
    acc_ref[...] = jnp.zeros_like(acc_ref)
    m_ref[...] = jnp.full(m_ref.shape, NEG_BIG, F32)
    l_ref[...] = jnp.zeros_like(l_ref)

    def scores_into(pair, dst_ref, hs):
        koff = pl.multiple_of(pair * tk, tk)
        for h in hs:
            dst_ref[h] = _dot(k_ref[0, pl.ds(koff, tk), h * LANES:(h + 1) * LANES], qm_ref[h])

    def softmax_pv(pair, src_ref, masked, hs, nblk=2):
        for h in hs:
            s = src_ref[h, :nblk * tq, :]
            if masked:
                kidx = pair * tk + lax.broadcasted_iota(I32, s.shape, 0)
                qidx = i * tq + lax.broadcasted_iota(I32, s.shape, 1) % tq
                s = jnp.where(kidx <= qidx, s, NEG_BIG)
            m_old = m_ref[h]
            m_new = jnp.maximum(m_old, jnp.max(s, axis=0, keepdims=True))
            alpha = jnp.exp2(m_old - m_new)
            p = jnp.exp2(s - m_new)
            l_ref[h] = alpha * l_ref[h] + jnp.sum(p, axis=0, keepdims=True)
            m_ref[h] = m_new
            pb = p.astype(BF16)
            pv = _dot(vt_ref[0, 2 * pair, h * LANES:(h + 1) * LANES, :], pb[:tq])
            if nblk == 2:
                pv = pv + _dot(vt_ref[0, 2 * pair + 1, h * LANES:(h + 1) * LANES, :], pb[tq:])
            acc_ref[h] = acc_ref[h] * alpha + pv

    bufs = (sa_ref, sb_ref)
    n_full = i // 2
    all_heads = tuple(range(heads))
    scores_into(0, sa_ref, all_heads)

    def body(n, carry):
        for par in range(2):
            @pl.when(n % 2 == par)
            def _():
                for h in all_heads:
                    scores_into(n + 1, bufs[1 - par], (h,))
                    softmax_pv(n, bufs[par], False, (h,))
        return carry

    lax.fori_loop(0, n_full, body, 0)
    for par in range(2):
        for odd in range(2):
            @pl.when((n_full % 2 == par) & (i % 2 == odd))
            def _():
                softmax_pv(n_full, bufs[par], True, all_heads, nblk=1 + odd)

    lv = lam_ref[...]
    lam = (jnp.exp(jnp.sum(lv[0:1] * lv[1:2], axis=1, keepdims=True))
           - jnp.exp(jnp.sum(lv[2:3] * lv[3:4], axis=1, keepdims=True)) + lambda_init)
    for h in range(heads):
        acc = acc_ref[h]
        l = l_ref[h]
        o = acc[:, :tq] / l[:, :tq] - lam * (acc[:, tq:] / l[:, tq:])
        ms = jnp.mean(o * o, axis=0, keepdims=True)
        o = o * lax.rsqrt(ms + NORM_EPS) * gain_ref[...] * (1.0 - lambda_init)
        o_ref[0, :, h * LANES:(h + 1) * LANES] = o.T.astype(o_ref.dtype)


def _attn_call(qt4, k, vt4, lam_vecs, head_gain, lambda_init, heads):
    bsz, nq, dq, tq = qt4.shape
    seq = k.shape[1]
    dv = vt4.shape[2]
    w = heads * LANES
    assert nq % 2 == 0, "kv blocks are consumed in pairs"
    return pl.pallas_call(
        functools.partial(_attn_kernel, tq=tq, heads=heads, lambda_init=lambda_init),
        grid=(bsz, ATTN_HEADS // heads, nq),
        in_specs=[
            pl.BlockSpec((1, 1, w, tq), lambda b, h, i: (b, i, h, 0)),
            pl.BlockSpec((1, seq, w), lambda b, h, i: (b, 0, h)),
            pl.BlockSpec((1, nq, w, tq), lambda b, h, i: (b, 0, h, 0)),
            pl.BlockSpec((4, ATTN_QK_DIM), lambda b, h, i: (0, 0)),
            pl.BlockSpec((ATTN_V_DIM, 1), lambda b, h, i: (0, 0)),
        ],
        out_specs=pl.BlockSpec((1, tq, w), lambda b, h, i: (b, i, h)),
        out_shape=jax.ShapeDtypeStruct((bsz, seq, dv), BF16),
        scratch_shapes=[pltpu.VMEM((heads, ATTN_V_DIM, 2 * tq), F32), pltpu.VMEM((heads, 1, 2 * tq), F32),
                        pltpu.VMEM((heads, 1, 2 * tq), F32), pltpu.VMEM((heads, LANES, 2 * tq), BF16),
                        pltpu.VMEM((heads, 2 * tq, 2 * tq), F32), pltpu.VMEM((heads, 2 * tq, 2 * tq), F32)],
        compiler_params=_cparams(("arbitrary", "arbitrary", "arbitrary")),
        name="diff_attn",
    )(qt4, k, vt4, lam_vecs, head_gain.reshape(ATTN_V_DIM, 1))


def _ssd_kernel(xbc_ref, dt_ref, z_ref, cw_ref, cb_ref, alog_ref, dskip_ref, nw_ref, expand_ref,
                y_ref, tail_ref, state_ref, *, n_heads):
    c = pl.program_id(1)
    L = xbc_ref.shape[1]
    d_inner = n_heads * SSD_HEAD_DIM
    gw = d_inner // SSD_GROUPS
    hpg = n_heads // SSD_GROUPS

    @pl.when(c == 0)
    def _():
        tail_ref[...] = jnp.zeros_like(tail_ref)
        state_ref[...] = jnp.zeros_like(state_ref)

    u = xbc_ref[0]
    tail = tail_ref[...]
    row8 = lax.broadcasted_iota(I32, tail.shape, 0)
    conv = u * cw_ref[SSD_CONV - 1:SSD_CONV, :] + cb_ref[...]
    for s in range(1, SSD_CONV):
        r = pltpu.roll(u, s, axis=0)
        top = jnp.where(row8 < s, pltpu.roll(tail, s, axis=0), r[:SUBLANES])
        shifted = jnp.concatenate([top, r[SUBLANES:]], axis=0)
        conv = conv + shifted * cw_ref[SSD_CONV - 1 - s:SSD_CONV - s, :]
    tail_ref[...] = u[L - SUBLANES:, :]
    xbc = _silu(conv)
    xs = xbc[:, :d_inner]
    bmat = xbc[:, d_inner:d_inner + SSD_GROUPS * SSD_STATE]
    cmat = xbc[:, d_inner + SSD_GROUPS * SSD_STATE:]

    dt = dt_ref[0]
    a = dt * (-jnp.exp(alog_ref[...]))
    ri = lax.broadcasted_iota(I32, (L, L), 0)
    ci = lax.broadcasted_iota(I32, (L, L), 1)
    causal = ri >= ci
    tril = jnp.where(causal, 1.0, 0.0).astype(BF16)
    a_hi, a_mid, a_lo = _split3(a)
    a_cum = _dot(tril, a_hi) + _dot(tril, a_mid) + _dot(tril, a_lo)
    a_cum_t = a_cum.T
    a_last = a_cum[L - 1:L, :]
    ea = jnp.exp(a_cum)
    to_end = jnp.exp(a_last - a_cum)

    expand = expand_ref[...]

    def widen(v):
        hi, lo = _split2(v)
        return _dot(hi, expand) + _dot(lo, expand)

    dt_w = widen(dt)
    ea_w = widen(ea)
    te_w = widen(to_end)
    xdt = xs * dt_w
    xdt_b = xdt.astype(BF16)
    xte_b = (xdt * te_w).astype(BF16)

    y_parts = []
    for g in range(SSD_GROUPS):
        cg = cmat[:, g * SSD_STATE:(g + 1) * SSD_STATE]
        bg = bmat[:, g * SSD_STATE:(g + 1) * SSD_STATE]
        cg_b = cg.astype(BF16)
        cb = _dot_nt(cg_b, bg.astype(BF16))
        st = state_ref[g]
        y_off = _dot(cg_b, st.astype(BF16)) * ea_w[:, g * gw:(g + 1) * gw]
        y_diag = []
        for e in range(hpg):
            hh = g * hpg + e
            seg = a_cum[:, hh:hh + 1] - a_cum_t[hh:hh + 1, :]
            decay = jnp.exp(jnp.where(causal, seg, -jnp.inf))
            m = (cb * decay).astype(BF16)
            y_diag.append(_dot(m, xdt_b[:, hh * SSD_HEAD_DIM:(hh + 1) * SSD_HEAD_DIM]))
        y_parts.append(jnp.concatenate(y_diag, axis=1) + y_off)
        bt = bg.T.astype(BF16)
        state_ref[g] = st * ea_w[L - 1:L, g * gw:(g + 1) * gw] + _dot(bt, xte_b[:, g * gw:(g + 1) * gw])

    y = jnp.concatenate(y_parts, axis=1) + dskip_ref[...] * xs
    y = y * _silu(z_ref[0])
    outs = []
    for g in range(SSD_GROUPS):
        yg = y[:, g * gw:(g + 1) * gw]
        ms = jnp.mean(yg * yg, axis=-1, keepdims=True)
        outs.append(yg * lax.rsqrt(ms + NORM_EPS) * nw_ref[:, g * gw:(g + 1) * gw])
    y_ref[0] = jnp.concatenate(outs, axis=1).astype(y_ref.dtype)


def _ssd_call(xbc, dt, z, conv_w, conv_b, a_log, d_skip, ssd_norm):
    bsz, seq, d_conv = xbc.shape
    n_heads = a_log.shape[0]
    d_inner = n_heads * SSD_HEAD_DIM
    L = SSD_CHUNK
    alog = jnp.pad(a_log.astype(F32), (0, LANES - n_heads)).reshape(1, LANES)
    dskip = jnp.repeat(d_skip.astype(F32), SSD_HEAD_DIM).reshape(1, d_inner)
    head_of_lane = jnp.arange(d_inner) // SSD_HEAD_DIM
    expand = (jnp.arange(LANES)[:, None] == head_of_lane[None, :]).astype(BF16)
    const2 = lambda b, c: (0, 0)
    return pl.pallas_call(
        functools.partial(_ssd_kernel, n_heads=n_heads),
        grid=(bsz, seq // L),
        in_specs=[
            pl.BlockSpec((1, L, d_conv), lambda b, c: (b, c, 0)),
            pl.BlockSpec((1, L, LANES), lambda b, c: (b, c, 0)),
            pl.BlockSpec((1, L, d_inner), lambda b, c: (b, c, 0)),
            pl.BlockSpec((SSD_CONV, d_conv), const2),
            pl.BlockSpec((1, d_conv), const2),
            pl.BlockSpec((1, LANES), const2),
            pl.BlockSpec((1, d_inner), const2),
            pl.BlockSpec((1, d_inner), const2),
            pl.BlockSpec((LANES, d_inner), const2),
        ],
        out_specs=pl.BlockSpec((1, L, d_inner), lambda b, c: (b, c, 0)),
        out_shape=jax.ShapeDtypeStruct((bsz, seq, d_inner), BF16),
        scratch_shapes=[
            pltpu.VMEM((SUBLANES, d_conv), F32),
            pltpu.VMEM((SSD_GROUPS, SSD_STATE, d_inner // SSD_GROUPS), F32),
        ],
        compiler_params=_cparams(("arbitrary", "arbitrary")),
        name="ssd_scan",
    )(xbc, dt, z, conv_w, conv_b.reshape(1, d_conv), alog, dskip, ssd_norm.reshape(1, d_inner), expand)


def _merge_kernel(x_ref, ya_ref, ys_ref, gates_ref, mod_ref, nf_ref,
                  wba_ref, wbs_ref, wo_ref, wr_ref, wsg_ref, wsu_ref, wsd_ref,
                  h2_ref, base_ref, sc_ref):
    d = x_ref.shape[2]
    gate_m = mod_ref[0, 2:3, :]
    shift_f = mod_ref[0, 3:4, :]
    scale_f = mod_ref[0, 4:5, :]
    gate_f = mod_ref[0, 5:6, :]
    gates = gates_ref[0]
    mixed = gates[:, :d] * _dot(ya_ref[0], wba_ref[...]) + gates[:, d:] * _dot(ys_ref[0], wbs_ref[...])
    x1 = x_ref[0] + gate_m * _dot(mixed.astype(BF16), wo_ref[...])
    ms = jnp.mean(x1 * x1, axis=-1, keepdims=True)
    h2 = (x1 * lax.rsqrt(ms + NORM_EPS) * nf_ref[...]) * (1.0 + scale_f) + shift_f
    h2b = h2.astype(BF16)
    h2_ref[0] = _to_tiles(h2b)
    sc_ref[0] = _sigmoid(_dot(h2b, wr_ref[...]))
    act = _silu(_dot(h2b, wsg_ref[...])) * _dot(h2b, wsu_ref[...])
    shared = _dot(act.astype(BF16), wsd_ref[...])
    base_ref[0] = x1 + gate_f * shared


def _merge_call(x, y_attn, y_ssd, gates, mod3, norm_ffn, w_ba, w_bs, w_out, w_router, w_sg, w_su, w_sd, tm):
    bsz, seq, d = x.shape
    ws = [w.astype(BF16) for w in (w_ba, w_bs, w_out, w_router, w_sg, w_su, w_sd)]
    const2 = lambda b, i: (0, 0)
    wspec = lambda w: pl.BlockSpec(w.shape, const2, pipeline_mode=pl.Buffered(1))
    row_spec = lambda n: pl.BlockSpec((1, tm, n), lambda b, i: (b, i, 0))
    return pl.pallas_call(
        _merge_kernel,
        grid=(bsz, seq // tm),
        in_specs=[row_spec(d), row_spec(y_attn.shape[2]), row_spec(y_ssd.shape[2]), row_spec(2 * d),
                  pl.BlockSpec((1, 6, d), lambda b, i: (b, 0, 0)), pl.BlockSpec((1, d), const2)]
                 + [wspec(w) for w in ws],
        out_specs=[pl.BlockSpec((1, tm, d // LANES, LANES), lambda b, i: (b, i, 0, 0)), row_spec(d),
                   row_spec(N_EXPERTS)],
        out_shape=[jax.ShapeDtypeStruct((bsz, seq, d // LANES, LANES), BF16),
                   jax.ShapeDtypeStruct((bsz, seq, d), F32), jax.ShapeDtypeStruct((bsz, seq, N_EXPERTS), F32)],
        compiler_params=_cparams(("arbitrary", "arbitrary")),
        name="merge_ffn_pre",
    )(x, y_attn, y_ssd, gates, mod3, norm_ffn.reshape(1, d), *ws)


def _first_max(v, iota, sentinel):
    mx = jnp.max(v, axis=0, keepdims=True)
    first = jnp.min(jnp.where(v == mx, iota, sentinel), axis=0, keepdims=True)
    return mx, first


def _route_kernel(sc_ref, bias_ref, idx_ref, w_ref):
    sc = sc_ref[...].T
    tm = sc.shape[1]
    choice = sc + bias_ref[...]
    per_group = N_EXPERTS // N_EXPERT_GROUPS
    gi = lax.broadcasted_iota(I32, (per_group, tm), 0)
    gscore = []
    for g in range(N_EXPERT_GROUPS):
        cg = choice[g * per_group:(g + 1) * per_group, :]
        m1, f1 = _first_max(cg, gi, per_group)
        m2 = jnp.max(jnp.where(gi == f1, -jnp.inf, cg), axis=0, keepdims=True)
        gscore.append(m1 + m2)
    cur = jnp.concatenate(gscore, axis=0)
    giota = lax.broadcasted_iota(I32, cur.shape, 0)
    keep = jnp.zeros(cur.shape, F32)
    for _ in range(TOPK_GROUPS):
        _, f = _first_max(cur, giota, N_EXPERT_GROUPS)
        sel = giota == f
        keep = jnp.where(sel, 1.0, keep)
        cur = jnp.where(sel, -jnp.inf, cur)
    masked = jnp.concatenate(
        [jnp.where(keep[g:g + 1, :] > 0.5, choice[g * per_group:(g + 1) * per_group, :], -jnp.inf)
         for g in range(N_EXPERT_GROUPS)], axis=0)
    eiota = lax.broadcasted_iota(I32, masked.shape, 0)
    ids, ws = [], []
    for _ in range(TOP_K):
        _, f = _first_max(masked, eiota, N_EXPERTS)
        sel = eiota == f
        ids.append(f)
        ws.append(jnp.sum(jnp.where(sel, sc, 0.0), axis=0, keepdims=True))
        masked = jnp.where(sel, -jnp.inf, masked)
    idx_ref[...] = jnp.concatenate(ids, axis=0)
    w = jnp.concatenate(ws, axis=0)
    w = w / (jnp.sum(w, axis=0, keepdims=True) + 1e-20) * ROUTED_SCALE
    cols = [jnp.broadcast_to(w[k:k + 1, :], (LANES, tm)).T for k in range(TOP_K)]
    w_ref[...] = _to_tiles(jnp.concatenate(cols, axis=1))


def _route_call(scores, router_bias, tm):
    n = scores.shape[0]
    return pl.pallas_call(
        _route_kernel,
        grid=(n // tm,),
        in_specs=[pl.BlockSpec((tm, N_EXPERTS), lambda i: (i, 0)), pl.BlockSpec((N_EXPERTS, 1), lambda i: (0, 0))],
        out_specs=[pl.BlockSpec((TOP_K, tm), lambda i: (0, i)), pl.BlockSpec((tm, TOP_K, LANES), lambda i: (i, 0, 0))],
        out_shape=[jax.ShapeDtypeStruct((TOP_K, n), I32), jax.ShapeDtypeStruct((n, TOP_K, LANES), F32)],
        compiler_params=_cparams(("arbitrary",)),
        name="route_topk",
    )(scores, router_bias.astype(F32).reshape(N_EXPERTS, 1))


def _rank_kernel(idx_ref, dest_ref, cnt_ref, base_ref, *, blk):
    phase = pl.program_id(0)
    i = pl.program_id(1)
    idx = idx_ref[...]
    tm = idx.shape[1]
    eiota = lax.broadcasted_iota(I32, (N_EXPERTS, tm), 0)
    onehot = jnp.zeros((N_EXPERTS, tm), F32)
    for k in range(TOP_K):
        onehot = onehot + jnp.where(eiota == idx[k:k + 1, :], 1.0, 0.0)

    @pl.when((phase == 0) & (i == 0))
    def _():
        base_ref[...] = jnp.zeros_like(base_ref)

    @pl.when((phase == 1) & (i == 0))
    def _():
        counts = base_ref[...]
        cnt_ref[...] = counts
        padded = jnp.floor((counts + (blk - 1)) * (1.0 / blk)) * blk
        ri = lax.broadcasted_iota(I32, (N_EXPERTS, N_EXPERTS), 0)
        ci = lax.broadcasted_iota(I32, (N_EXPERTS, N_EXPERTS), 1)
        lower = jnp.where(ci < ri, 1.0, 0.0).astype(BF16)
        hi, mid, lo = _split3(padded)
        base_ref[...] = _dot(lower, hi) + _dot(lower, mid) + _dot(lower, lo)

    @pl.when(phase == 1)
    def _():
        ri = lax.broadcasted_iota(I32, (tm, tm), 0)
        ci = lax.broadcasted_iota(I32, (tm, tm), 1)
        before = jnp.where(ri < ci, 1.0, 0.0).astype(BF16)
        prior = _dot(onehot.astype(BF16), before) + base_ref[:, 0:1]
        rows = []
        for k in range(TOP_K):
            rows.append(jnp.sum(jnp.where(eiota == idx[k:k + 1, :], prior, 0.0), axis=0, keepdims=True))
        dest_ref[...] = jnp.concatenate(rows, axis=0).astype(I32)

    base_ref[...] = base_ref[...] + jnp.sum(onehot, axis=1, keepdims=True)


def _rank_call(idx, tm, blk):
    n = idx.shape[1]
    return pl.pallas_call(
        functools.partial(_rank_kernel, blk=blk),
        grid=(2, n // tm),
        in_specs=[pl.BlockSpec((TOP_K, tm), lambda p, i: (0, i))],
        out_specs=[pl.BlockSpec((TOP_K, tm), lambda p, i: (0, i * p)),
                   pl.BlockSpec((N_EXPERTS, LANES), lambda p, i: (0, 0))],
        out_shape=[jax.ShapeDtypeStruct((TOP_K, n), I32), jax.ShapeDtypeStruct((N_EXPERTS, LANES), F32)],
        scratch_shapes=[pltpu.VMEM((N_EXPERTS, LANES), F32)],
        compiler_params=_cparams(("arbitrary", "arbitrary")),
        name="expert_rank",
    )(idx)


def _dispatch_kernel(pend_ref, padded_ref, nu_ref, dest_ref, h_ref, xs_ref, zero_ref, sem_z, sem_s, *, blk):
    i = pl.program_id(0)
    tm = h_ref.shape[0]
    n_blocks = xs_ref.shape[0] // blk

    def zero_copy(start):
        return pltpu.make_async_copy(zero_ref, xs_ref.at[pl.ds(pl.multiple_of(start, blk), blk)], sem_z)

    @pl.when(i == 0)
    def _():
        zero_ref[...] = jnp.zeros_like(zero_ref)

        def start(e, _):
            @pl.when(padded_ref[e] > 0)
            def _():
                zero_copy(pend_ref[e] - blk).start()
            return 0

        def wait(e, _):
            @pl.when(padded_ref[e] > 0)
            def _():
                zero_copy(pend_ref[e] - blk).wait()
            return 0

        def start_idle(b, _):
            zero_copy(b * blk).start()
            return 0

        def wait_idle(b, _):
            zero_copy(b * blk).wait()
            return 0

        lax.fori_loop(0, N_EXPERTS, start, 0)
        lax.fori_loop(nu_ref[0], n_blocks, start_idle, 0)
        lax.fori_loop(0, N_EXPERTS, wait, 0)
        lax.fori_loop(nu_ref[0], n_blocks, wait_idle, 0)

    def issue(g, _):
        for u in range(ISSUE_UNROLL):
            t = g * ISSUE_UNROLL + u
            for k in range(TOP_K):
                pltpu.make_async_copy(h_ref.at[t], xs_ref.at[dest_ref[k, t]], sem_s).start(priority=k % 2)
        return 0

    lax.fori_loop(0, tm // ISSUE_UNROLL, issue, 0)
    for k in range(TOP_K):
        pltpu.make_async_copy(h_ref, xs_ref.at[pl.ds(0, tm)], sem_s).wait()


def _dispatch_call(pend, padded, n_used, dest, h2t, n_slots, blk, tm):
    n, s, lanes = h2t.shape
    return pl.pallas_call(
        functools.partial(_dispatch_kernel, blk=blk),
        grid_spec=pltpu.PrefetchScalarGridSpec(
            num_scalar_prefetch=3,
            grid=(n // tm,),
            in_specs=[
                pl.BlockSpec((TOP_K, tm), lambda i, pe, pa, nu: (0, i), memory_space=pltpu.SMEM),
                pl.BlockSpec((tm, s, lanes), lambda i, pe, pa, nu: (i, 0, 0)),
            ],
            out_specs=pl.BlockSpec(memory_space=pl.ANY),
            scratch_shapes=[pltpu.VMEM((blk, s, lanes), h2t.dtype), pltpu.SemaphoreType.DMA(()),
                            pltpu.SemaphoreType.DMA(())],
        ),
        out_shape=jax.ShapeDtypeStruct((n_slots, s, lanes), h2t.dtype),
        compiler_params=_cparams(("arbitrary",)),
        name="moe_dispatch",
    )(pend, padded, n_used, dest, h2t)


def _expert_kernel(be_ref, nu_ref, first_ref, slot_ref, nxt_ref, xs_ref, wg_hbm, wu_hbm, wd_hbm, y_ref,
                   wg_f, wu_f, wd_f, wg_b, wu_b, wd_b, sems):
    i = pl.program_id(0)

    def weight_copies(e, slot):
        return (pltpu.make_async_copy(wg_hbm.at[e], wg_f.at[slot], sems.at[slot]),
                pltpu.make_async_copy(wu_hbm.at[e], wu_f.at[slot], sems.at[slot]),
                pltpu.make_async_copy(wd_hbm.at[e], wd_f.at[slot], sems.at[slot]))

    @pl.when(i < nu_ref[0])
    def _():
        @pl.when(first_ref[i] == 1)
        def _():
            slot = slot_ref[i]

            @pl.when(i == 0)
            def _():
                for cp in weight_copies(be_ref[0], slot):
                    cp.start()

            @pl.when(nxt_ref[i] >= 0)
            def _():
                for cp in weight_copies(nxt_ref[i], 1 - slot):
                    cp.start()

            for cp in weight_copies(be_ref[i], slot):
                cp.wait()
            wg_b[...] = wg_f[slot].astype(BF16)
            wu_b[...] = wu_f[slot].astype(BF16)
            wd_b[...] = wd_f[slot].astype(BF16)

        xb = _from_tiles(xs_ref[...])
        act = _silu(_dot(xb, wg_b[...])) * _dot(xb, wu_b[...])
        y_ref[...] = _to_tiles(_dot(act.astype(BF16), wd_b[...]).astype(y_ref.dtype))

    @pl.when(i >= nu_ref[0])
    def _():
        y_ref[...] = jnp.zeros_like(y_ref)


def _expert_call(block_e, n_used, first, slot, nxt, xs, w_gate, w_up, w_down, blk):
    n_slots, s, lanes = xs.shape
    _, d, ff = w_gate.shape
    n_blocks = n_slots // blk
    blk_map = lambda i, be, nu, fi, sl, nx: (jnp.minimum(i, nu[0] - 1), 0, 0)
    hbm = pl.BlockSpec(memory_space=pl.ANY)
    return pl.pallas_call(
        _expert_kernel,
        grid_spec=pltpu.PrefetchScalarGridSpec(
            num_scalar_prefetch=5,
            grid=(n_blocks,),
            in_specs=[pl.BlockSpec((blk, s, lanes), blk_map), hbm, hbm, hbm],
            out_specs=pl.BlockSpec((blk, s, lanes), lambda i, be, nu, fi, sl, nx: (i, 0, 0)),
            scratch_shapes=[pltpu.VMEM((2, d, ff), F32), pltpu.VMEM((2, d, ff), F32), pltpu.VMEM((2, ff, d), F32),
                            pltpu.VMEM((d, ff), BF16), pltpu.VMEM((d, ff), BF16), pltpu.VMEM((ff, d), BF16),
                            pltpu.SemaphoreType.DMA((2,))],
        ),
        out_shape=jax.ShapeDtypeStruct((n_slots, s, lanes), BF16),
        compiler_params=_cparams(("arbitrary",)),
        name="moe_experts",
    )(block_e, n_used, first, slot, nxt, xs, w_gate, w_up, w_down)


def _combine_kernel(dest_ref, w_ref, base_ref, gf_ref, nfin_ref, y_ref, o_ref, buf_ref, sem, *, final_norm):
    tc = base_ref.shape[0]

    def issue(g, _):
        for u in range(ISSUE_UNROLL):
            t = g * ISSUE_UNROLL + u
            for k in range(TOP_K):
                pltpu.make_async_copy(y_ref.at[dest_ref[k, t]], buf_ref.at[k, t], sem).start(priority=k % 2)
        return 0

    lax.fori_loop(0, tc // ISSUE_UNROLL, issue, 0)
    for k in range(TOP_K):
        pltpu.make_async_copy(y_ref.at[pl.ds(0, tc)], buf_ref.at[k], sem).wait()
    routed = buf_ref[0].astype(F32) * w_ref[:, 0:1, :]
    for k in range(1, TOP_K):
        routed = routed + buf_ref[k].astype(F32) * w_ref[:, k:k + 1, :]
    x2 = base_ref[...] + gf_ref[0] * _from_tiles(routed)
    if final_norm:
        ms = jnp.mean(x2 * x2, axis=-1, keepdims=True)
        x2 = x2 * lax.rsqrt(ms + NORM_EPS) * nfin_ref[...]
    o_ref[...] = x2


def _combine_call(dest, w_tok, base, gate_f, norm_final, y, tokens_per_batch, tc, final_norm):
    n, d = base.shape
    steps_per_batch = tokens_per_batch // tc
    return pl.pallas_call(
        functools.partial(_combine_kernel, final_norm=final_norm),
        grid=(n // tc,),
        in_specs=[
            pl.BlockSpec((TOP_K, tc), lambda i: (0, i), memory_space=pltpu.SMEM),
            pl.BlockSpec((tc, TOP_K, LANES), lambda i: (i, 0, 0)),
            pl.BlockSpec((tc, d), lambda i: (i, 0)),
            pl.BlockSpec((1, 1, d), lambda i: (i // steps_per_batch, 0, 0)),
            pl.BlockSpec((1, d), lambda i: (0, 0)),
            pl.BlockSpec(memory_space=pl.ANY),
        ],
        out_specs=pl.BlockSpec((tc, d), lambda i: (i, 0)),
        out_shape=jax.ShapeDtypeStruct((n, d), F32),
        scratch_shapes=[pltpu.VMEM((TOP_K, tc, d // LANES, LANES), y.dtype), pltpu.SemaphoreType.DMA(())],
        compiler_params=_cparams(("arbitrary",)),
        name="moe_combine",
    )(dest, w_tok, base, gate_f, norm_final.reshape(1, d), y)


def _moe_blocks(n_pairs, blk):
    return -(-n_pairs // blk) + N_EXPERTS


def _moe_layout(counts, n_pairs, blk):
    n_blocks = _moe_blocks(n_pairs, blk)
    padded = (counts + blk - 1) // blk * blk
    pend = jnp.cumsum(padded)
    n_used = pend[-1] // blk
    blk_start = jnp.arange(n_blocks, dtype=I32) * blk
    block_e = jnp.sum((pend[None, :] <= blk_start[:, None]).astype(I32), axis=1)
    block_e = jnp.minimum(block_e, N_EXPERTS - 1)
    used = jnp.arange(n_blocks) < n_used
    prev_e = jnp.concatenate([jnp.full((1,), -1, I32), block_e[:-1]])
    first = (used & (block_e != prev_e)).astype(I32)
    slot = (jnp.cumsum(first) - 1) % 2
    idx_first = jnp.where(first == 1, jnp.arange(n_blocks, dtype=I32), n_blocks)
    nxt_pos = lax.cummin(jnp.concatenate([idx_first[1:], jnp.full((1,), n_blocks, I32)]), reverse=True)
    nxt = jnp.where(nxt_pos < n_blocks, block_e[jnp.minimum(nxt_pos, n_blocks - 1)], -1)
    return (pend.astype(I32), padded.astype(I32), block_e.astype(I32), n_used.reshape(1).astype(I32),
            first, slot.astype(I32), nxt.astype(I32), n_blocks)


def kernel(x, c, positions, w_ada, b_ada, norm_mix, w_in, b_gate, lambda_q1, lambda_k1, lambda_q2, lambda_k2,
           attn_head_norm, conv_w, conv_b, dt_bias, a_log, d_skip, ssd_norm, w_branch_attn, w_branch_ssd, w_out,
           norm_ffn, w_router, router_bias, w_exp_gate, w_exp_up, w_exp_down, w_sh_gate, w_sh_up, w_sh_down,
           norm_final):
    bsz, seq, d = x.shape
    n = bsz * seq
    depth = w_ada.shape[0]
    tm = min(TOKEN_TILE, seq)
    c_pad = jnp.pad(c, ((0, -bsz % SUBLANES), (0, 0)))
    for l in range(depth):
        last = l == depth - 1
        mod3 = _ada_call(c_pad, w_ada[l], b_ada[l])[:bsz].reshape(bsz, 6, d)
        qt4, k, vt4, z, xbc, gates, dt = _inproj_call(x, positions, mod3, norm_mix[l], b_gate[l], dt_bias[l],
                                                      w_in[l], tm)
        lambda_init = 0.8 - 0.6 * math.exp(-0.3 * l)
        lam_vecs = jnp.stack([lambda_q1[l], lambda_k1[l], lambda_q2[l], lambda_k2[l]]).astype(F32)
        y_attn = _attn_call(qt4, k, vt4, lam_vecs, attn_head_norm[l].astype(F32), lambda_init, ATTN_HEADS_PER_STEP)
        y_ssd = _ssd_call(xbc, dt, z, conv_w[l], conv_b[l], a_log[l], d_skip[l], ssd_norm[l])
        h2, base, scores = _merge_call(x, y_attn, y_ssd, gates, mod3, norm_ffn[l], w_branch_attn[l],
                                       w_branch_ssd[l], w_out[l], w_router[l], w_sh_gate[l], w_sh_up[l],
                                       w_sh_down[l], tm)
        idx, w_tiles = _route_call(scores.reshape(n, N_EXPERTS), router_bias[l], tm)
        dest, cnt = _rank_call(idx, tm, MOE_BLOCK)
        pend, padded, block_e, n_used, first, slot, nxt, n_blocks = _moe_layout(cnt[:, 0].astype(I32), n * TOP_K,
                                                                                MOE_BLOCK)
        xs = _dispatch_call(pend, padded, n_used, dest, h2.reshape(n, d // LANES, LANES), n_blocks * MOE_BLOCK,
                            MOE_BLOCK, tm)
        y = _expert_call(block_e, n_used, first, slot, nxt, xs, w_exp_gate[l], w_exp_up[l], w_exp_down[l], MOE_BLOCK)
        out = _combine_call(dest, w_tiles, base.reshape(n, d), mod3[:, 5:6, :], norm_final, y, seq,
                            min(COMBINE_TILE, seq), final_norm=last)
        x = out.reshape(bsz, seq, d)
    return x
```

```python
import functools
import math

import jax
import jax.numpy as jnp
from jax import lax
from jax.experimental import pallas as pl
from jax.experimental.pallas import tpu as pltpu

F32 = jnp.float32
BF16 = jnp.bfloat16
I32 = jnp.int32

ATTN_HEADS = 8
ATTN_QK_DIM = 64
ATTN_V_DIM = 128
ROPE_DIM = 16
ROPE_THETA = 500000.0
SSD_HEAD_DIM = 64
SSD_GROUPS = 4
SSD_STATE = 128
SSD_CONV = 4
SSD_CHUNK = 128
N_EXPERTS = 256
TOP_K = 8
N_EXPERT_GROUPS = 8
TOPK_GROUPS = 4
ROUTED_SCALE = 2.5
NORM_EPS = 1e-6
LOG2E = 1.4426950408889634
NEG_BIG = -1e30

LANES = 128
SUBLANES = 8
VMEM_LIMIT_BYTES = 56 * 1024 * 1024

TOKEN_TILE = 256
MOE_BLOCK = 256
ROUTE_TILE = 1024
COMBINE_TILE = 128
ATTN_HEADS_PER_STEP = 4
ISSUE_UNROLL = 4


def _cparams(sem, vmem=VMEM_LIMIT_BYTES):
    return pltpu.CompilerParams(dimension_semantics=sem, vmem_limit_bytes=vmem)


def _dot(a, b):
    return jnp.dot(a, b, preferred_element_type=F32)


def _dot_nt(a, b):
    return lax.dot_general(a, b, (((1,), (1,)), ((), ())), preferred_element_type=F32)


def _sigmoid(x):
    return 1.0 / (1.0 + jnp.exp(-x))


def _silu(x):
    return x * _sigmoid(x)


def _to_tiles(x):
    return x.reshape(x.shape[0], x.shape[1] // LANES, LANES)


def _from_tiles(x):
    return x.reshape(x.shape[0], x.shape[1] * x.shape[2])


def _split2(x):
    hi = x.astype(BF16)
    lo = (x - hi.astype(F32)).astype(BF16)
    return hi, lo


def _split3(x):
    hi = x.astype(BF16)
    r = x - hi.astype(F32)
    mid = r.astype(BF16)
    lo = (r - mid.astype(F32)).astype(BF16)
    return hi, mid, lo


def _ada_kernel(c_ref, w_ref, b_ref, o_ref):
    o_ref[...] = _dot(_silu(c_ref[...]).astype(BF16), w_ref[...].astype(BF16)) + b_ref[...]


def _ada_call(c_pad, w_ada, b_ada):
    rows, d = c_pad.shape
    n = w_ada.shape[1]
    tn = 1024
    return pl.pallas_call(
        _ada_kernel,
        grid=(n // tn,),
        in_specs=[
            pl.BlockSpec((rows, d), lambda j: (0, 0)),
            pl.BlockSpec((d, tn), lambda j: (0, j)),
            pl.BlockSpec((1, tn), lambda j: (0, j)),
        ],
        out_specs=pl.BlockSpec((rows, tn), lambda j: (0, j)),
        out_shape=jax.ShapeDtypeStruct((rows, n), F32),
        compiler_params=_cparams(("arbitrary",)),
        name="ada_mod",
    )(c_pad, w_ada, b_ada.reshape(1, n))


def _inproj_kernel(x_ref, pos_ref, mod_ref, nm_ref, freq_ref, sign_ref, dtb_ref, bg_ref,
                   wq_ref, wk_ref, wv_ref, wz_ref, wx_ref, wg_ref, wdt_ref,
                   qt_ref, k_ref, vt_ref, z_ref, xbc_ref, gates_ref, dt_ref, *, q_scale):
    x = x_ref[0]
    ms = jnp.mean(x * x, axis=-1, keepdims=True)
    shift = mod_ref[0, 0:1, :]
    scale = mod_ref[0, 1:2, :]
    h = (x * lax.rsqrt(ms + NORM_EPS) * nm_ref[...]) * (1.0 + scale) + shift
    hb = h.astype(BF16)

    ang = pos_ref[0].astype(F32) * freq_ref[...]
    cos = jnp.cos(ang)
    sin_signed = jnp.sin(ang) * sign_ref[...]
    lane = lax.broadcasted_iota(I32, ang.shape, 1) % ATTN_QK_DIM
    first_half = lane < (ROPE_DIM // 2)

    def rope(t):
        outs = []
        for hd in range(t.shape[1] // LANES):
            th = t[:, hd * LANES:(hd + 1) * LANES]
            up = pltpu.roll(th, LANES - ROPE_DIM // 2, axis=1)
            down = pltpu.roll(th, ROPE_DIM // 2, axis=1)
            partner = jnp.where(first_half, up, down)
            outs.append(th * cos + partner * sin_signed)
        return jnp.concatenate(outs, axis=1)

    q = rope(_dot(hb, wq_ref[...])) * q_scale
    qt_ref[0, 0] = q.T.astype(BF16)
    k = rope(_dot(hb, wk_ref[...]))
    k_ref[0] = k.astype(BF16)
    v = _dot(hb, wv_ref[...])
    vt_ref[0, 0] = v.T.astype(BF16)
    z_ref[0] = _dot(hb, wz_ref[...])
    xbc_ref[0] = _dot(hb, wx_ref[...])
    g = _dot(hb, wg_ref[...]) + bg_ref[...]
    gates_ref[0] = _sigmoid(g)
    dtr = _dot(hb, wdt_ref[...]) + dtb_ref[...]
    dt_ref[0] = jnp.maximum(dtr, 0.0) + jnp.log1p(jnp.exp(-jnp.abs(dtr)))


def _inproj_call(x, positions, mod3, norm_mix, b_gate, dt_bias, w_in, tm):
    bsz, seq, d = x.shape
    dq = ATTN_HEADS * 2 * ATTN_QK_DIM
    dv = ATTN_HEADS * ATTN_V_DIM
    n_heads_ssd = dt_bias.shape[0]
    d_inner = n_heads_ssd * SSD_HEAD_DIM
    d_conv = d_inner + 2 * SSD_GROUPS * SSD_STATE
    sizes = (dq, dq, dv, d_inner, d_conv, n_heads_ssd, 2 * d)
    offs = [0]
    for s in sizes:
        offs.append(offs[-1] + s)
    wb = w_in.astype(BF16)
    wq, wk, wv, wz, wx, wdt, wg = (wb[:, offs[i]:offs[i + 1]] for i in range(7))
    wdt = jnp.pad(wdt, ((0, 0), (0, LANES - n_heads_ssd)))
    dtb = jnp.pad(dt_bias.astype(F32), (0, LANES - n_heads_ssd)).reshape(1, LANES)

    half = ROPE_DIM // 2
    lane_d = jnp.arange(LANES) % ATTN_QK_DIM
    inv_freq = 1.0 / (ROPE_THETA ** (jnp.arange(0, ROPE_DIM, 2, dtype=F32) / ROPE_DIM))
    freq = jnp.where(lane_d < ROPE_DIM, inv_freq[lane_d % half], 0.0).reshape(1, LANES).astype(F32)
    sign = jnp.where(lane_d < half, -1.0, jnp.where(lane_d < ROPE_DIM, 1.0, 0.0)).reshape(1, LANES).astype(F32)

    const2 = lambda b, i: (0, 0)
    wspec = lambda w: pl.BlockSpec(w.shape, const2, pipeline_mode=pl.Buffered(1))
    row_spec = lambda n: pl.BlockSpec((1, tm, n), lambda b, i: (b, i, 0))
    col_spec = lambda n: pl.BlockSpec((1, 1, n, tm), lambda b, i: (b, i, 0, 0))
    q_scale = (ATTN_QK_DIM ** -0.5) * LOG2E
    return pl.pallas_call(
        functools.partial(_inproj_kernel, q_scale=q_scale),
        grid=(bsz, seq // tm),
        in_specs=[
            row_spec(d),
            pl.BlockSpec((1, tm, 1), lambda b, i: (b, i, 0)),
            pl.BlockSpec((1, 6, d), lambda b, i: (b, 0, 0)),
            pl.BlockSpec((1, d), const2),
            pl.BlockSpec((1, LANES), const2),
            pl.BlockSpec((1, LANES), const2),
            pl.BlockSpec((1, LANES), const2),
            pl.BlockSpec((1, 2 * d), const2),
            wspec(wq), wspec(wk), wspec(wv), wspec(wz), wspec(wx), wspec(wg), wspec(wdt),
        ],
        out_specs=[col_spec(dq), row_spec(dq), col_spec(dv), row_spec(d_inner), row_spec(d_conv),
                   row_spec(2 * d), row_spec(LANES)],
        out_shape=[
            jax.ShapeDtypeStruct((bsz, seq // tm, dq, tm), BF16),
            jax.ShapeDtypeStruct((bsz, seq, dq), BF16),
            jax.ShapeDtypeStruct((bsz, seq // tm, dv, tm), BF16),
            jax.ShapeDtypeStruct((bsz, seq, d_inner), F32),
            jax.ShapeDtypeStruct((bsz, seq, d_conv), F32),
            jax.ShapeDtypeStruct((bsz, seq, 2 * d), F32),
            jax.ShapeDtypeStruct((bsz, seq, LANES), F32),
        ],
        compiler_params=_cparams(("arbitrary", "arbitrary")),
        name="in_proj",
    )(x, positions.reshape(bsz, seq, 1), mod3, norm_mix.reshape(1, d), freq, sign, dtb,
      b_gate.reshape(1, 2 * d), wq, wk, wv, wz, wx, wg, wdt)


def _attn_kernel(qt_ref, k_ref, vt_ref, lam_ref, gain_ref, o_ref, acc_ref, m_ref, l_ref, qm_ref, sa_ref, sb_ref,
                 *, tq, heads, lambda_init):
    i = pl.program_id(2)
    tk = 2 * tq
    row = lax.broadcasted_iota(I32, (LANES, tq), 0)
    for h in range(heads):
        qt = qt_ref[0, 0, h * LANES:(h + 1) * LANES, :]
        zero = jnp.zeros_like(qt)
        qm_ref[h] = jnp.concatenate([jnp.where(row < ATTN_QK_DIM, qt, zero),
                                     jnp.where(row >= ATTN_QK_DIM, qt, zero)], axis=1)
    acc_ref[...] = jnp.zeros_like(acc_ref)
    m_ref[...] = jnp.full(m_ref.shape, NEG_BIG, F32)
    l_ref[...] = jnp.zeros_like(l_ref)

    def scores_into(pair, dst_ref, hs):
        koff = pl.multiple_of(pair * tk, tk)
        for h in hs:
            dst_ref[h] = _dot(k_ref[0, pl.ds(koff, tk), h * LANES:(h + 1) * LANES], qm_ref[h])

    def softmax_pv(pair, src_ref, masked, hs, nblk=2):
        for h in hs:
            s = src_ref[h, :nblk * tq, :]
            if masked:
                kidx = pair * tk + lax.broadcasted_iota(I32, s.shape, 0)
                qidx = i * tq + lax.broadcasted_iota(I32, s.shape, 1) % tq
                s = jnp.where(kidx <= qidx, s, NEG_BIG)
            m_old = m_ref[h]
            m_new = jnp.maximum(m_old, jnp.max(s, axis=0, keepdims=True))
            alpha = jnp.exp2(m_old - m_new)
            p = jnp.exp2(s - m_new)
            l_ref[h] = alpha * l_ref[h] + jnp.sum(p, axis=0, keepdims=True)
            m_ref[h] = m_new
            pb = p.astype(BF16)
            pv = _dot(vt_ref[0, 2 * pair, h * LANES:(h + 1) * LANES, :], pb[:tq])
            if nblk == 2:
                pv = pv + _dot(vt_ref[0, 2 * pair + 1, h * LANES:(h + 1) * LANES, :], pb[tq:])
            acc_ref[h] = acc_ref[h] * alpha + pv

    bufs = (sa_ref, sb_ref)
    n_full = i // 2
    all_heads = tuple(range(heads))
    scores_into(0, sa_ref, all_heads)

    def body(n, carry):
        for par in range(2):
            @pl.when(n % 2 == par)
            def _():
                for h in all_heads:
                    scores_into(n + 1, bufs[1 - par], (h,))
                    softmax_pv(n, bufs[par], False, (h,))
        return carry

    lax.fori_loop(0, n_full, body, 0)
    for par in range(2):
        for odd in range(2):
            @pl.when((n_full % 2 == par) & (i % 2 == odd))
            def _():
                softmax_pv(n_full, bufs[par], True, all_heads, nblk=1 + odd)

    lv = lam_ref[...]
    lam = (jnp.exp(jnp.sum(lv[0:1] * lv[1:2], axis=1, keepdims=True))
           - jnp.exp(jnp.sum(lv[2:3] * lv[3:4], axis=1, keepdims=True)) + lambda_init)
    for h in range(heads):
        acc = acc_ref[h]
        l = l_ref[h]
        o = acc[:, :tq] / l[:, :tq] - lam * (acc[:, tq:] / l[:, tq:])
        ms = jnp.mean(o * o, axis=0, keepdims=True)
        o = o * lax.rsqrt(ms + NORM_EPS) * gain_ref[...] * (1.0 - lambda_init)
        o_ref[0, :, h * LANES:(h + 1) * LANES] = o.T.astype(o_ref.dtype)


def _attn_call(qt4, k, vt4, lam_vecs, head_gain, lambda_init, heads):
    bsz, nq, dq, tq = qt4.shape
    seq = k.shape[1]
    dv = vt4.shape[2]
    w = heads * LANES
    assert nq % 2 == 0, "kv blocks are consumed in pairs"
    return pl.pallas_call(
        functools.partial(_attn_kernel, tq=tq, heads=heads, lambda_init=lambda_init),
        grid=(bsz, ATTN_HEADS // heads, nq),
        in_specs=[
            pl.BlockSpec((1, 1, w, tq), lambda b, h, i: (b, i, h, 0)),
            pl.BlockSpec((1, seq, w), lambda b, h, i: (b, 0, h)),
            pl.BlockSpec((1, nq, w, tq), lambda b, h, i: (b, 0, h, 0)),
            pl.BlockSpec((4, ATTN_QK_DIM), lambda b, h, i: (0, 0)),
            pl.BlockSpec((ATTN_V_DIM, 1), lambda b, h, i: (0, 0)),
        ],
        out_specs=pl.BlockSpec((1, tq, w), lambda b, h, i: (b, i, h)),
        out_shape=jax.ShapeDtypeStruct((bsz, seq, dv), BF16),
        scratch_shapes=[pltpu.VMEM((heads, ATTN_V_DIM, 2 * tq), F32), pltpu.VMEM((heads, 1, 2 * tq), F32),
                        pltpu.VMEM((heads, 1, 2 * tq), F32), pltpu.VMEM((heads, LANES, 2 * tq), BF16),
                        pltpu.VMEM((heads, 2 * tq, 2 * tq), F32), pltpu.VMEM((heads, 2 * tq, 2 * tq), F32)],
        compiler_params=_cparams(("arbitrary", "arbitrary", "arbitrary")),
        name="diff_attn",
    )(qt4, k, vt4, lam_vecs, head_gain.reshape(ATTN_V_DIM, 1))


def _ssd_kernel(xbc_ref, dt_ref, z_ref, cw_ref, cb_ref, alog_ref, dskip_ref, nw_ref, expand_ref,
                y_ref, tail_ref, state_ref, *, n_heads):
    c = pl.program_id(1)
    L = xbc_ref.shape[1]
    d_inner = n_heads * SSD_HEAD_DIM
    gw = d_inner // SSD_GROUPS
    hpg = n_heads // SSD_GROUPS

    @pl.when(c == 0)
    def _():
        tail_ref[...] = jnp.zeros_like(tail_ref)
        state_ref[...] = jnp.zeros_like(state_ref)

    u = xbc_ref[0]
    tail = tail_ref[...]
    row8 = lax.broadcasted_iota(I32, tail.shape, 0)
    conv = u * cw_ref[SSD_CONV - 1:SSD_CONV, :] + cb_ref[...]
    for s in range(1, SSD_CONV):
        r = pltpu.roll(u, s, axis=0)
        top = jnp.where(row8 < s, pltpu.roll(tail, s, axis=0), r[:SUBLANES])
        shifted = jnp.concatenate([top, r[SUBLANES:]], axis=0)
        conv = conv + shifted * cw_ref[SSD_CONV - 1 - s:SSD_CONV - s, :]
    tail_ref[...] = u[L - SUBLANES:, :]
    xbc = _silu(conv)
    xs = xbc[:, :d_inner]
    bmat = xbc[:, d_inner:d_inner + SSD_GROUPS * SSD_STATE]
    cmat = xbc[:, d_inner + SSD_GROUPS * SSD_STATE:]

    dt = dt_ref[0]
    a = dt * (-jnp.exp(alog_ref[...]))
    ri = lax.broadcasted_iota(I32, (L, L), 0)
    ci = lax.broadcasted_iota(I32, (L, L), 1)
    causal = ri >= ci
    tril = jnp.where(causal, 1.0, 0.0).astype(BF16)
    a_hi, a_mid, a_lo = _split3(a)
    a_cum = _dot(tril, a_hi) + _dot(tril, a_mid) + _dot(tril, a_lo)
    a_cum_t = a_cum.T
    a_last = a_cum[L - 1:L, :]
    ea = jnp.exp(a_cum)
    to_end = jnp.exp(a_last - a_cum)

    expand = expand_ref[...]

    def widen(v):
        hi, lo = _split2(v)
        return _dot(hi, expand) + _dot(lo, expand)

    dt_w = widen(dt)
    ea_w = widen(ea)
    te_w = widen(to_end)
    xdt = xs * dt_w
    xdt_b = xdt.astype(BF16)
    xte_b = (xdt * te_w).astype(BF16)

    y_parts = []
    for g in range(SSD_GROUPS):
        cg = cmat[:, g * SSD_STATE:(g + 1) * SSD_STATE]
        bg = bmat[:, g * SSD_STATE:(g + 1) * SSD_STATE]
        cg_b = cg.astype(BF16)
        cb = _dot_nt(cg_b, bg.astype(BF16))
        st = state_ref[g]
        y_off = _dot(cg_b, st.astype(BF16)) * ea_w[:, g * gw:(g + 1) * gw]
        y_diag = []
        for e in range(hpg):
            hh = g * hpg + e
            seg = a_cum[:, hh:hh + 1] - a_cum_t[hh:hh + 1, :]
            decay = jnp.exp(jnp.where(causal, seg, -jnp.inf))
            m = (cb * decay).astype(BF16)
            y_diag.append(_dot(m, xdt_b[:, hh * SSD_HEAD_DIM:(hh + 1) * SSD_HEAD_DIM]))
        y_parts.append(jnp.concatenate(y_diag, axis=1) + y_off)
        bt = bg.T.astype(BF16)
        state_ref[g] = st * ea_w[L - 1:L, g * gw:(g + 1) * gw] + _dot(bt, xte_b[:, g * gw:(g + 1) * gw])

    y = jnp.concatenate(y_parts, axis=1) + dskip_ref[...] * xs
    y = y * _silu(z_ref[0])
    outs = []
    for g in range(SSD_GROUPS):
        yg = y[:, g * gw:(g + 1) * gw]
        ms = jnp.mean(yg * yg, axis=-1, keepdims=True)
        outs.append(yg * lax.rsqrt(ms + NORM_EPS) * nw_ref[:, g * gw:(g + 1) * gw])
    y_ref[0] = jnp.concatenate(outs, axis=1).astype(y_ref.dtype)


def _ssd_call(xbc, dt, z, conv_w, conv_b, a_log, d_skip, ssd_norm):
    bsz, seq, d_conv = xbc.shape
    n_heads = a_log.shape[0]
    d_inner = n_heads * SSD_HEAD_DIM
    L = SSD_CHUNK
    alog = jnp.pad(a_log.astype(F32), (0, LANES - n_heads)).reshape(1, LANES)
    dskip = jnp.repeat(d_skip.astype(F32), SSD_HEAD_DIM).reshape(1, d_inner)
    head_of_lane = jnp.arange(d_inner) // SSD_HEAD_DIM
    expand = (jnp.arange(LANES)[:, None] == head_of_lane[None, :]).astype(BF16)
    const2 = lambda b, c: (0, 0)
    return pl.pallas_call(
        functools.partial(_ssd_kernel, n_heads=n_heads),
        grid=(bsz, seq // L),
        in_specs=[
            pl.BlockSpec((1, L, d_conv), lambda b, c: (b, c, 0)),
            pl.BlockSpec((1, L, LANES), lambda b, c: (b, c, 0)),
            pl.BlockSpec((1, L, d_inner), lambda b, c: (b, c, 0)),
            pl.BlockSpec((SSD_CONV, d_conv), const2),
            pl.BlockSpec((1, d_conv), const2),
            pl.BlockSpec((1, LANES), const2),
            pl.BlockSpec((1, d_inner), const2),
            pl.BlockSpec((1, d_inner), const2),
            pl.BlockSpec((LANES, d_inner), const2),
        ],
        out_specs=pl.BlockSpec((1, L, d_inner), lambda b, c: (b, c, 0)),
        out_shape=jax.ShapeDtypeStruct((bsz, seq, d_inner), BF16),
        scratch_shapes=[
            pltpu.VMEM((SUBLANES, d_conv), F32),
            pltpu.VMEM((SSD_GROUPS, SSD_STATE, d_inner // SSD_GROUPS), F32),
        ],
        compiler_params=_cparams(("arbitrary", "arbitrary")),
        name="ssd_scan",
    )(xbc, dt, z, conv_w, conv_b.reshape(1, d_conv), alog, dskip, ssd_norm.reshape(1, d_inner), expand)


def _merge_kernel(x_ref, ya_ref, ys_ref, gates_ref, mod_ref, nf_ref,
                  wba_ref, wbs_ref, wo_ref, wr_ref, wsg_ref, wsu_ref, wsd_ref,
                  h2_ref, base_ref, sc_ref):
    d = x_ref.shape[2]
    gate_m = mod_ref[0, 2:3, :]
    shift_f = mod_ref[0, 3:4, :]
    scale_f = mod_ref[0, 4:5, :]
    gate_f = mod_ref[0, 5:6, :]
    gates = gates_ref[0]
    mixed = gates[:, :d] * _dot(ya_ref[0], wba_ref[...]) + gates[:, d:] * _dot(ys_ref[0], wbs_ref[...])
    x1 = x_ref[0] + gate_m * _dot(mixed.astype(BF16), wo_ref[...])
    ms = jnp.mean(x1 * x1, axis=-1, keepdims=True)
    h2 = (x1 * lax.rsqrt(ms + NORM_EPS) * nf_ref[...]) * (1.0 + scale_f) + shift_f
    h2b = h2.astype(BF16)
    h2_ref[0] = _to_tiles(h2b)
    sc_ref[0] = _sigmoid(_dot(h2b, wr_ref[...]))
    act = _silu(_dot(h2b, wsg_ref[...])) * _dot(h2b, wsu_ref[...])
    shared = _dot(act.astype(BF16), wsd_ref[...])
    base_ref[0] = x1 + gate_f * shared


def _merge_call(x, y_attn, y_ssd, gates, mod3, norm_ffn, w_ba, w_bs, w_out, w_router, w_sg, w_su, w_sd, tm):
    bsz, seq, d = x.shape
    ws = [w.astype(BF16) for w in (w_ba, w_bs, w_out, w_router, w_sg, w_su, w_sd)]
    const2 = lambda b, i: (0, 0)
    wspec = lambda w: pl.BlockSpec(w.shape, const2, pipeline_mode=pl.Buffered(1))
    row_spec = lambda n: pl.BlockSpec((1, tm, n), lambda b, i: (b, i, 0))
    return pl.pallas_call(
        _merge_kernel,
        grid=(bsz, seq // tm),
        in_specs=[row_spec(d), row_spec(y_attn.shape[2]), row_spec(y_ssd.shape[2]), row_spec(2 * d),
                  pl.BlockSpec((1, 6, d), lambda b, i: (b, 0, 0)), pl.BlockSpec((1, d), const2)]
                 + [wspec(w) for w in ws],
        out_specs=[pl.BlockSpec((1, tm, d // LANES, LANES), lambda b, i: (b, i, 0, 0)), row_spec(d),
                   row_spec(N_EXPERTS)],
        out_shape=[jax.ShapeDtypeStruct((bsz, seq, d // LANES, LANES), BF16),
                   jax.ShapeDtypeStruct((bsz, seq, d), F32), jax.ShapeDtypeStruct((bsz, seq, N_EXPERTS), F32)],
        compiler_params=_cparams(("arbitrary", "arbitrary")),
        name="merge_ffn_pre",
    )(x, y_attn, y_ssd, gates, mod3, norm_ffn.reshape(1, d), *ws)


def _first_max(v, iota, sentinel):
    mx = jnp.max(v, axis=0, keepdims=True)
    first = jnp.min(jnp.where(v == mx, iota, sentinel), axis=0, keepdims=True)
    return mx, first


def _route_kernel(sc_ref, bias_ref, idx_ref, w_ref):
    sc = sc_ref[...].T
    tm = sc.shape[1]
    choice = sc + bias_ref[...]
    per_group = N_EXPERTS // N_EXPERT_GROUPS
    gi = lax.broadcasted_iota(I32, (per_group, tm), 0)
    gscore = []
    for g in range(N_EXPERT_GROUPS):
        cg = choice[g * per_group:(g + 1) * per_group, :]
        m1, f1 = _first_max(cg, gi, per_group)
        m2 = jnp.max(jnp.where(gi == f1, -jnp.inf, cg), axis=0, keepdims=True)
        gscore.append(m1 + m2)
    cur = jnp.concatenate(gscore, axis=0)
    giota = lax.broadcasted_iota(I32, cur.shape, 0)
    keep = jnp.zeros(cur.shape, F32)
    for _ in range(TOPK_GROUPS):
        _, f = _first_max(cur, giota, N_EXPERT_GROUPS)
        sel = giota == f
        keep = jnp.where(sel, 1.0, keep)
        cur = jnp.where(sel, -jnp.inf, cur)
    masked = jnp.concatenate(
        [jnp.where(keep[g:g + 1, :] > 0.5, choice[g * per_group:(g + 1) * per_group, :], -jnp.inf)
         for g in range(N_EXPERT_GROUPS)], axis=0)
    eiota = lax.broadcasted_iota(I32, masked.shape, 0)
    ids, ws = [], []
    for _ in range(TOP_K):
        _, f = _first_max(masked, eiota, N_EXPERTS)
        sel = eiota == f
        ids.append(f)
        ws.append(jnp.sum(jnp.where(sel, sc, 0.0), axis=0, keepdims=True))
        masked = jnp.where(sel, -jnp.inf, masked)
    idx_ref[...] = jnp.concatenate(ids, axis=0)
    w = jnp.concatenate(ws, axis=0)
    w = w / (jnp.sum(w, axis=0, keepdims=True) + 1e-20) * ROUTED_SCALE
    cols = [jnp.broadcast_to(w[k:k + 1, :], (LANES, tm)).T for k in range(TOP_K)]
    w_ref[...] = _to_tiles(jnp.concatenate(cols, axis=1))


def _route_call(scores, router_bias, tm):
    n = scores.shape[0]
    return pl.pallas_call(
        _route_kernel,
        grid=(n // tm,),
        in_specs=[pl.BlockSpec((tm, N_EXPERTS), lambda i: (i, 0)), pl.BlockSpec((N_EXPERTS, 1), lambda i: (0, 0))],
        out_specs=[pl.BlockSpec((TOP_K, tm), lambda i: (0, i)), pl.BlockSpec((tm, TOP_K, LANES), lambda i: (i, 0, 0))],
        out_shape=[jax.ShapeDtypeStruct((TOP_K, n), I32), jax.ShapeDtypeStruct((n, TOP_K, LANES), F32)],
        compiler_params=_cparams(("arbitrary",)),
        name="route_topk",
    )(scores, router_bias.astype(F32).reshape(N_EXPERTS, 1))


def _rank_kernel(idx_ref, dest_ref, cnt_ref, base_ref, *, blk):
    phase = pl.program_id(0)
    i = pl.program_id(1)
    idx = idx_ref[...]
    tm = idx.shape[1]
    eiota = lax.broadcasted_iota(I32, (N_EXPERTS, tm), 0)
    onehot = jnp.zeros((N_EXPERTS, tm), F32)
    for k in range(TOP_K):
        onehot = onehot + jnp.where(eiota == idx[k:k + 1, :], 1.0, 0.0)

    @pl.when((phase == 0) & (i == 0))
    def _():
        base_ref[...] = jnp.zeros_like(base_ref)

    @pl.when((phase == 1) & (i == 0))
    def _():
        counts = base_ref[...]
        cnt_ref[...] = counts
        padded = jnp.floor((counts + (blk - 1)) * (1.0 / blk)) * blk
        ri = lax.broadcasted_iota(I32, (N_EXPERTS, N_EXPERTS), 0)
        ci = lax.broadcasted_iota(I32, (N_EXPERTS, N_EXPERTS), 1)
        lower = jnp.where(ci < ri, 1.0, 0.0).astype(BF16)
        hi, mid, lo = _split3(padded)
        base_ref[...] = _dot(lower, hi) + _dot(lower, mid) + _dot(lower, lo)

    @pl.when(phase == 1)
    def _():
        ri = lax.broadcasted_iota(I32, (tm, tm), 0)
        ci = lax.broadcasted_iota(I32, (tm, tm), 1)
        before = jnp.where(ri < ci, 1.0, 0.0).astype(BF16)
        prior = _dot(onehot.astype(BF16), before) + base_ref[:, 0:1]
        rows = []
        for k in range(TOP_K):
            rows.append(jnp.sum(jnp.where(eiota == idx[k:k + 1, :], prior, 0.0), axis=0, keepdims=True))
        dest_ref[...] = jnp.concatenate(rows, axis=0).astype(I32)

    base_ref[...] = base_ref[...] + jnp.sum(onehot, axis=1, keepdims=True)


def _rank_call(idx, tm, blk):
    n = idx.shape[1]
    return pl.pallas_call(
        functools.partial(_rank_kernel, blk=blk),
        grid=(2, n // tm),
        in_specs=[pl.BlockSpec((TOP_K, tm), lambda p, i: (0, i))],
        out_specs=[pl.BlockSpec((TOP_K, tm), lambda p, i: (0, i * p)),
                   pl.BlockSpec((N_EXPERTS, LANES), lambda p, i: (0, 0))],
        out_shape=[jax.ShapeDtypeStruct((TOP_K, n), I32), jax.ShapeDtypeStruct((N_EXPERTS, LANES), F32)],
        scratch_shapes=[pltpu.VMEM((N_EXPERTS, LANES), F32)],
        compiler_params=_cparams(("arbitrary", "arbitrary")),
        name="expert_rank",
    )(idx)


def _dispatch_kernel(pend_ref, padded_ref, nu_ref, dest_ref, h_ref, xs_ref, zero_ref, sem_z, sem_s, *, blk):
    i = pl.program_id(0)
    tm = h_ref.shape[0]
    n_blocks = xs_ref.shape[0] // blk

    def zero_copy(start):
        return pltpu.make_async_copy(zero_ref, xs_ref.at[pl.ds(pl.multiple_of(start, blk), blk)], sem_z)

    @pl.when(i == 0)
    def _():
        zero_ref[...] = jnp.zeros_like(zero_ref)

        def start(e, _):
            @pl.when(padded_ref[e] > 0)
            def _():
                zero_copy(pend_ref[e] - blk).start()
            return 0

        def wait(e, _):
            @pl.when(padded_ref[e] > 0)
            def _():
                zero_copy(pend_ref[e] - blk).wait()
            return 0

        def start_idle(b, _):
            zero_copy(b * blk).start()
            return 0

        def wait_idle(b, _):
            zero_copy(b * blk).wait()
            return 0

        lax.fori_loop(0, N_EXPERTS, start, 0)
        lax.fori_loop(nu_ref[0], n_blocks, start_idle, 0)
        lax.fori_loop(0, N_EXPERTS, wait, 0)
        lax.fori_loop(nu_ref[0], n_blocks, wait_idle, 0)

    def issue(g, _):
        for u in range(ISSUE_UNROLL):
            t = g * ISSUE_UNROLL + u
            for k in range(TOP_K):
                pltpu.make_async_copy(h_ref.at[t], xs_ref.at[dest_ref[k, t]], sem_s).start(priority=k % 2)
        return 0

    lax.fori_loop(0, tm // ISSUE_UNROLL, issue, 0)
    for k in range(TOP_K):
        pltpu.make_async_copy(h_ref, xs_ref.at[pl.ds(0, tm)], sem_s).wait()


def _dispatch_call(pend, padded, n_used, dest, h2t, n_slots, blk, tm):
    n, s, lanes = h2t.shape
    return pl.pallas_call(
        functools.partial(_dispatch_kernel, blk=blk),
        grid_spec=pltpu.PrefetchScalarGridSpec(
            num_scalar_prefetch=3,
            grid=(n // tm,),
            in_specs=[
                pl.BlockSpec((TOP_K, tm), lambda i, pe, pa, nu: (0, i), memory_space=pltpu.SMEM),
                pl.BlockSpec((tm, s, lanes), lambda i, pe, pa, nu: (i, 0, 0)),
            ],
            out_specs=pl.BlockSpec(memory_space=pl.ANY),
            scratch_shapes=[pltpu.VMEM((blk, s, lanes), h2t.dtype), pltpu.SemaphoreType.DMA(()),
                            pltpu.SemaphoreType.DMA(())],
        ),
        out_shape=jax.ShapeDtypeStruct((n_slots, s, lanes), h2t.dtype),
        compiler_params=_cparams(("arbitrary",)),
        name="moe_dispatch",
    )(pend, padded, n_used, dest, h2t)


RING = 3


def _expert_kernel(be_ref, nu_ref, first_ref, wslot_ref, nxt1_ref, nxt2_ref, xs_hbm, wg_hbm, wu_hbm, wd_hbm, y_hbm,
                   xbuf, ybuf, zbuf, wg_f, wu_f, wd_f, wg_b, wu_b, wd_b, x_sems, y_sems, w_sems, *, blk):
    i = pl.program_id(0)
    nu = nu_ref[0]
    last = pl.num_programs(0) - 1

    def x_copy(b):
        slot = b % RING
        return pltpu.make_async_copy(xs_hbm.at[pl.ds(pl.multiple_of(b * blk, blk), blk)], xbuf.at[slot],
                                     x_sems.at[slot])

    def y_copy(b, src):
        return pltpu.make_async_copy(src, y_hbm.at[pl.ds(pl.multiple_of(b * blk, blk), blk)], y_sems.at[b % RING])

    def weight_copies(e, slot):
        return (pltpu.make_async_copy(wg_hbm.at[e], wg_f.at[slot], w_sems.at[slot]),
                pltpu.make_async_copy(wu_hbm.at[e], wu_f.at[slot], w_sems.at[slot]),
                pltpu.make_async_copy(wd_hbm.at[e], wd_f.at[slot], w_sems.at[slot]))

    @pl.when(i == 0)
    def _():
        zbuf[...] = jnp.zeros_like(zbuf)
        x_copy(0).start()

        @pl.when(nu > 1)
        def _():
            x_copy(1).start()

        for cp in weight_copies(be_ref[0], 0):
            cp.start()

        @pl.when(nxt1_ref[0] >= 0)
        def _():
            for cp in weight_copies(nxt1_ref[0], 1):
                cp.start()

    @pl.when(i < nu)
    def _():
        @pl.when(i + 2 < nu)
        def _():
            x_copy(i + 2).start()

        @pl.when(first_ref[i] == 1)
        def _():
            slot = wslot_ref[i]

            @pl.when(nxt2_ref[i] >= 0)
            def _():
                for cp in weight_copies(nxt2_ref[i], (slot + 2) % RING):
                    cp.start()

            for cp in weight_copies(be_ref[i], slot):
                cp.wait()
            wg_b[...] = wg_f[slot].astype(BF16)
            wu_b[...] = wu_f[slot].astype(BF16)
            wd_b[...] = wd_f[slot].astype(BF16)

        x_copy(i).wait()
        slot = i % RING
        xb = _from_tiles(xbuf[slot])
        act = _silu(_dot(xb, wg_b[...])) * _dot(xb, wu_b[...])
        ybuf[slot] = _to_tiles(_dot(act.astype(BF16), wd_b[...]).astype(ybuf.dtype))
        y_copy(i, ybuf.at[slot]).start()

    @pl.when(i >= nu)
    def _():
        y_copy(i, zbuf).start()

    @pl.when(i >= 2)
    def _():
        y_copy(i - 2, zbuf).wait()

    @pl.when(i == last)
    def _():
        @pl.when(i >= 1)
        def _():
            y_copy(i - 1, zbuf).wait()

        y_copy(i, zbuf).wait()


def _expert_call(block_e, n_used, first, wslot, nxt1, nxt2, xs, w_gate, w_up, w_down, blk):
    n_slots, s, lanes = xs.shape
    _, d, ff = w_gate.shape
    n_blocks = n_slots // blk
    hbm = pl.BlockSpec(memory_space=pl.ANY)
    return pl.pallas_call(
        functools.partial(_expert_kernel, blk=blk),
        grid_spec=pltpu.PrefetchScalarGridSpec(
            num_scalar_prefetch=6,
            grid=(n_blocks,),
            in_specs=[hbm, hbm, hbm, hbm],
            out_specs=hbm,
            scratch_shapes=[pltpu.VMEM((RING, blk, s, lanes), xs.dtype), pltpu.VMEM((RING, blk, s, lanes), BF16),
                            pltpu.VMEM((blk, s, lanes), BF16),
                            pltpu.VMEM((RING, d, ff), F32), pltpu.VMEM((RING, d, ff), F32),
                            pltpu.VMEM((RING, ff, d), F32),
                            pltpu.VMEM((d, ff), BF16), pltpu.VMEM((d, ff), BF16), pltpu.VMEM((ff, d), BF16),
                            pltpu.SemaphoreType.DMA((RING,)), pltpu.SemaphoreType.DMA((RING,)),
                            pltpu.SemaphoreType.DMA((RING,))],
        ),
        out_shape=jax.ShapeDtypeStruct((n_slots, s, lanes), BF16),
        compiler_params=_cparams(("arbitrary",)),
        name="moe_experts",
    )(block_e, n_used, first, wslot, nxt1, nxt2, xs, w_gate, w_up, w_down)


def _combine_kernel(dest_ref, dnext_ref, w_ref, base_ref, gf_ref, nfin_ref, y_ref, o_ref, buf_ref, sems, *,
                    final_norm):
    i = pl.program_id(0)
    last = pl.num_programs(0) - 1
    tc = base_ref.shape[0]

    def issue_all(d_ref, slot):
        def issue(g, _):
            for u in range(ISSUE_UNROLL):
                t = g * ISSUE_UNROLL + u
                for k in range(TOP_K):
                    pltpu.make_async_copy(y_ref.at[d_ref[k, t]], buf_ref.at[slot, k, t],
                                          sems.at[slot]).start(priority=k % 2)
            return 0

        lax.fori_loop(0, tc // ISSUE_UNROLL, issue, 0)

    @pl.when(i == 0)
    def _():
        issue_all(dest_ref, 0)

    for par in range(2):
        @pl.when((i < last) & (i % 2 == par))
        def _():
            issue_all(dnext_ref, 1 - par)

    for par in range(2):
        @pl.when(i % 2 == par)
        def _():
            for k in range(TOP_K):
                pltpu.make_async_copy(y_ref.at[pl.ds(0, tc)], buf_ref.at[par, k], sems.at[par]).wait()

    slot = i % 2
    routed = buf_ref[slot, 0].astype(F32) * w_ref[:, 0:1, :]
    for k in range(1, TOP_K):
        routed = routed + buf_ref[slot, k].astype(F32) * w_ref[:, k:k + 1, :]
    x2 = base_ref[...] + gf_ref[0] * _from_tiles(routed)
    if final_norm:
        ms = jnp.mean(x2 * x2, axis=-1, keepdims=True)
        x2 = x2 * lax.rsqrt(ms + NORM_EPS) * nfin_ref[...]
    o_ref[...] = x2


def _combine_call(dest, w_tok, base, gate_f, norm_final, y, tokens_per_batch, tc, final_norm):
    n, d = base.shape
    steps_per_batch = tokens_per_batch // tc
    steps = n // tc
    return pl.pallas_call(
        functools.partial(_combine_kernel, final_norm=final_norm),
        grid=(steps,),
        in_specs=[
            pl.BlockSpec((TOP_K, tc), lambda i: (0, i), memory_space=pltpu.SMEM),
            pl.BlockSpec((TOP_K, tc), lambda i: (0, jnp.minimum(i + 1, steps - 1)), memory_space=pltpu.SMEM),
            pl.BlockSpec((tc, TOP_K, LANES), lambda i: (i, 0, 0)),
            pl.BlockSpec((tc, d), lambda i: (i, 0)),
            pl.BlockSpec((1, 1, d), lambda i: (i // steps_per_batch, 0, 0)),
            pl.BlockSpec((1, d), lambda i: (0, 0)),
            pl.BlockSpec(memory_space=pl.ANY),
        ],
        out_specs=pl.BlockSpec((tc, d), lambda i: (i, 0)),
        out_shape=jax.ShapeDtypeStruct((n, d), F32),
        scratch_shapes=[pltpu.VMEM((2, TOP_K, tc, d // LANES, LANES), y.dtype), pltpu.SemaphoreType.DMA((2,))],
        compiler_params=_cparams(("arbitrary",)),
        name="moe_combine",
    )(dest, dest, w_tok, base, gate_f, norm_final.reshape(1, d), y)


def _moe_blocks(n_pairs, blk):
    return -(-n_pairs // blk) + N_EXPERTS


def _moe_layout(counts, n_pairs, blk):
    n_blocks = _moe_blocks(n_pairs, blk)
    padded = (counts + blk - 1) // blk * blk
    pend = jnp.cumsum(padded)
    n_used = pend[-1] // blk
    blk_start = jnp.arange(n_blocks, dtype=I32) * blk
    block_e = jnp.sum((pend[None, :] <= blk_start[:, None]).astype(I32), axis=1)
    block_e = jnp.minimum(block_e, N_EXPERTS - 1)
    used = jnp.arange(n_blocks) < n_used
    prev_e = jnp.concatenate([jnp.full((1,), -1, I32), block_e[:-1]])
    first = (used & (block_e != prev_e)).astype(I32)
    has = padded > 0
    ids = jnp.arange(N_EXPERTS, dtype=I32)
    ordinal = jnp.cumsum(has.astype(I32)) - 1
    at_or_after = lax.cummin(jnp.where(has, ids, N_EXPERTS), axis=0, reverse=True)
    after = jnp.concatenate([at_or_after[1:], jnp.full((2,), N_EXPERTS, I32)])
    nxt1_e = after[:N_EXPERTS]
    nxt2_e = after[jnp.minimum(nxt1_e, N_EXPERTS)]
    fix = lambda v: jnp.where(v >= N_EXPERTS, -1, v).astype(I32)
    wslot = (ordinal % RING).astype(I32)[block_e]
    nxt1 = fix(nxt1_e)[block_e]
    nxt2 = fix(nxt2_e)[block_e]
    return (pend.astype(I32), padded.astype(I32), block_e.astype(I32), n_used.reshape(1).astype(I32),
            first, wslot, nxt1, nxt2, n_blocks)


def kernel(x, c, positions, w_ada, b_ada, norm_mix, w_in, b_gate, lambda_q1, lambda_k1, lambda_q2, lambda_k2,
           attn_head_norm, conv_w, conv_b, dt_bias, a_log, d_skip, ssd_norm, w_branch_attn, w_branch_ssd, w_out,
           norm_ffn, w_router, router_bias, w_exp_gate, w_exp_up, w_exp_down, w_sh_gate, w_sh_up, w_sh_down,
           norm_final):
    bsz, seq, d = x.shape
    n = bsz * seq
    depth = w_ada.shape[0]
    tm = min(TOKEN_TILE, seq)
    c_pad = jnp.pad(c, ((0, -bsz % SUBLANES), (0, 0)))
    for l in range(depth):
        last = l == depth - 1
        mod3 = _ada_call(c_pad, w_ada[l], b_ada[l])[:bsz].reshape(bsz, 6, d)
        qt4, k, vt4, z, xbc, gates, dt = _inproj_call(x, positions, mod3, norm_mix[l], b_gate[l], dt_bias[l],
                                                      w_in[l], tm)
        lambda_init = 0.8 - 0.6 * math.exp(-0.3 * l)
        lam_vecs = jnp.stack([lambda_q1[l], lambda_k1[l], lambda_q2[l], lambda_k2[l]]).astype(F32)
        y_attn = _attn_call(qt4, k, vt4, lam_vecs, attn_head_norm[l].astype(F32), lambda_init, ATTN_HEADS_PER_STEP)
        y_ssd = _ssd_call(xbc, dt, z, conv_w[l], conv_b[l], a_log[l], d_skip[l], ssd_norm[l])
        h2, base, scores = _merge_call(x, y_attn, y_ssd, gates, mod3, norm_ffn[l], w_branch_attn[l],
                                       w_branch_ssd[l], w_out[l], w_router[l], w_sh_gate[l], w_sh_up[l],
                                       w_sh_down[l], tm)
        rt = min(ROUTE_TILE, n)
        idx, w_tiles = _route_call(scores.reshape(n, N_EXPERTS), router_bias[l], rt)
        dest, cnt = _rank_call(idx, rt, MOE_BLOCK)
        pend, padded, block_e, n_used, first, wslot, nxt1, nxt2, n_blocks = _moe_layout(
            cnt[:, 0].astype(I32), n * TOP_K, MOE_BLOCK)
        xs = _dispatch_call(pend, padded, n_used, dest, h2.reshape(n, d // LANES, LANES), n_blocks * MOE_BLOCK,
                            MOE_BLOCK, tm)
        y = _expert_call(block_e, n_used, first, wslot, nxt1, nxt2, xs, w_exp_gate[l], w_exp_up[l], w_exp_down[l],
                         MOE_BLOCK)
        out = _combine_call(dest, w_tiles, base.reshape(n, d), mod3[:, 5:6, :], norm_final, y, seq,
                            min(COMBINE_TILE, seq), final_norm=last)
        x = out.reshape(bsz, seq, d)
    return x
```

```python
import functools
import math

import jax
import jax.numpy as jnp
from jax import lax
from jax.experimental import pallas as pl
from jax.experimental.pallas import tpu as pltpu

F32 = jnp.float32
BF16 = jnp.bfloat16
I32 = jnp.int32

ATTN_HEADS = 8
ATTN_QK_DIM = 64
ATTN_V_DIM = 128
ROPE_DIM = 16
ROPE_THETA = 500000.0
SSD_HEAD_DIM = 64
SSD_GROUPS = 4
SSD_STATE = 128
SSD_CONV = 4
SSD_CHUNK = 128
N_EXPERTS = 256
TOP_K = 8
N_EXPERT_GROUPS = 8
TOPK_GROUPS = 4
ROUTED_SCALE = 2.5
NORM_EPS = 1e-6
LOG2E = 1.4426950408889634
NEG_BIG = -1e30

LANES = 128
SUBLANES = 8
VMEM_LIMIT_BYTES = 56 * 1024 * 1024

TOKEN_TILE = 256
MOE_BLOCK = 256
ROUTE_TILE = 1024
COMBINE_TILE = 128
ATTN_HEADS_PER_STEP = 4
ISSUE_UNROLL = 4


def _cparams(sem, vmem=VMEM_LIMIT_BYTES):
    return pltpu.CompilerParams(dimension_semantics=sem, vmem_limit_bytes=vmem)


def _dot(a, b):
    return jnp.dot(a, b, preferred_element_type=F32)


def _dot_nt(a, b):
    return lax.dot_general(a, b, (((1,), (1,)), ((), ())), preferred_element_type=F32)


def _sigmoid(x):
    return 1.0 / (1.0 + jnp.exp(-x))


def _silu(x):
    return x * _sigmoid(x)


def _to_tiles(x):
    return x.reshape(x.shape[0], x.shape[1] // LANES, LANES)


def _from_tiles(x):
    return x.reshape(x.shape[0], x.shape[1] * x.shape[2])


def _split2(x):
    hi = x.astype(BF16)
    lo = (x - hi.astype(F32)).astype(BF16)
    return hi, lo


def _split3(x):
    hi = x.astype(BF16)
    r = x - hi.astype(F32)
    mid = r.astype(BF16)
    lo = (r - mid.astype(F32)).astype(BF16)
    return hi, mid, lo


def _ada_kernel(c_ref, w_ref, b_ref, o_ref):
    o_ref[...] = _dot(_silu(c_ref[...]).astype(BF16), w_ref[...].astype(BF16)) + b_ref[...]


def _ada_call(c_pad, w_ada, b_ada):
    rows, d = c_pad.shape
    n = w_ada.shape[1]
    tn = 1024
    return pl.pallas_call(
        _ada_kernel,
        grid=(n // tn,),
        in_specs=[
            pl.BlockSpec((rows, d), lambda j: (0, 0)),
            pl.BlockSpec((d, tn), lambda j: (0, j)),
            pl.BlockSpec((1, tn), lambda j: (0, j)),
        ],
        out_specs=pl.BlockSpec((rows, tn), lambda j: (0, j)),
        out_shape=jax.ShapeDtypeStruct((rows, n), F32),
        compiler_params=_cparams(("arbitrary",)),
        name="ada_mod",
    )(c_pad, w_ada, b_ada.reshape(1, n))


def _inproj_kernel(x_ref, pos_ref, mod_ref, nm_ref, freq_ref, sign_ref, dtb_ref, bg_ref,
                   wq_ref, wk_ref, wv_ref, wz_ref, wx_ref, wg_ref, wdt_ref,
                   qt_ref, k_ref, vt_ref, z_ref, xbc_ref, gates_ref, dt_ref, *, q_scale):
    x = x_ref[0]
    ms = jnp.mean(x * x, axis=-1, keepdims=True)
    shift = mod_ref[0, 0:1, :]
    scale = mod_ref[0, 1:2, :]
    h = (x * lax.rsqrt(ms + NORM_EPS) * nm_ref[...]) * (1.0 + scale) + shift
    hb = h.astype(BF16)

    ang = pos_ref[0].astype(F32) * freq_ref[...]
    cos = jnp.cos(ang)
    sin_signed = jnp.sin(ang) * sign_ref[...]
    lane = lax.broadcasted_iota(I32, ang.shape, 1) % ATTN_QK_DIM
    first_half = lane < (ROPE_DIM // 2)

    def rope(t):
        outs = []
        for hd in range(t.shape[1] // LANES):
            th = t[:, hd * LANES:(hd + 1) * LANES]
            up = pltpu.roll(th, LANES - ROPE_DIM // 2, axis=1)
            down = pltpu.roll(th, ROPE_DIM // 2, axis=1)
            partner = jnp.where(first_half, up, down)
            outs.append(th * cos + partner * sin_signed)
        return jnp.concatenate(outs, axis=1)

    q = rope(_dot(hb, wq_ref[...])) * q_scale
    qt_ref[0, 0] = q.T.astype(BF16)
    k = rope(_dot(hb, wk_ref[...]))
    k_ref[0] = k.astype(BF16)
    v = _dot(hb, wv_ref[...])
    vt_ref[0, 0] = v.T.astype(BF16)
    z_ref[0] = _dot(hb, wz_ref[...])
    xbc_ref[0] = _dot(hb, wx_ref[...])
    g = _dot(hb, wg_ref[...]) + bg_ref[...]
    gates_ref[0] = _sigmoid(g)
    dtr = _dot(hb, wdt_ref[...]) + dtb_ref[...]
    dt_ref[0] = jnp.maximum(dtr, 0.0) + jnp.log1p(jnp.exp(-jnp.abs(dtr)))


def _inproj_call(x, positions, mod3, norm_mix, b_gate, dt_bias, w_in, tm):
    bsz, seq, d = x.shape
    dq = ATTN_HEADS * 2 * ATTN_QK_DIM
    dv = ATTN_HEADS * ATTN_V_DIM
    n_heads_ssd = dt_bias.shape[0]
    d_inner = n_heads_ssd * SSD_HEAD_DIM
    d_conv = d_inner + 2 * SSD_GROUPS * SSD_STATE
    sizes = (dq, dq, dv, d_inner, d_conv, n_heads_ssd, 2 * d)
    offs = [0]
    for s in sizes:
        offs.append(offs[-1] + s)
    wb = w_in.astype(BF16)
    wq, wk, wv, wz, wx, wdt, wg = (wb[:, offs[i]:offs[i + 1]] for i in range(7))
    wdt = jnp.pad(wdt, ((0, 0), (0, LANES - n_heads_ssd)))
    dtb = jnp.pad(dt_bias.astype(F32), (0, LANES - n_heads_ssd)).reshape(1, LANES)

    half = ROPE_DIM // 2
    lane_d = jnp.arange(LANES) % ATTN_QK_DIM
    inv_freq = 1.0 / (ROPE_THETA ** (jnp.arange(0, ROPE_DIM, 2, dtype=F32) / ROPE_DIM))
    freq = jnp.where(lane_d < ROPE_DIM, inv_freq[lane_d % half], 0.0).reshape(1, LANES).astype(F32)
    sign = jnp.where(lane_d < half, -1.0, jnp.where(lane_d < ROPE_DIM, 1.0, 0.0)).reshape(1, LANES).astype(F32)

    const2 = lambda b, i: (0, 0)
    wspec = lambda w: pl.BlockSpec(w.shape, const2, pipeline_mode=pl.Buffered(1))
    row_spec = lambda n: pl.BlockSpec((1, tm, n), lambda b, i: (b, i, 0))
    col_spec = lambda n: pl.BlockSpec((1, 1, n, tm), lambda b, i: (b, i, 0, 0))
    q_scale = (ATTN_QK_DIM ** -0.5) * LOG2E
    return pl.pallas_call(
        functools.partial(_inproj_kernel, q_scale=q_scale),
        grid=(bsz, seq // tm),
        in_specs=[
            row_spec(d),
            pl.BlockSpec((1, tm, 1), lambda b, i: (b, i, 0)),
            pl.BlockSpec((1, 6, d), lambda b, i: (b, 0, 0)),
            pl.BlockSpec((1, d), const2),
            pl.BlockSpec((1, LANES), const2),
            pl.BlockSpec((1, LANES), const2),
            pl.BlockSpec((1, LANES), const2),
            pl.BlockSpec((1, 2 * d), const2),
            wspec(wq), wspec(wk), wspec(wv), wspec(wz), wspec(wx), wspec(wg), wspec(wdt),
        ],
        out_specs=[col_spec(dq), row_spec(dq), col_spec(dv), row_spec(d_inner), row_spec(d_conv),
                   row_spec(2 * d), row_spec(LANES)],
        out_shape=[
            jax.ShapeDtypeStruct((bsz, seq // tm, dq, tm), BF16),
            jax.ShapeDtypeStruct((bsz, seq, dq), BF16),
            jax.ShapeDtypeStruct((bsz, seq // tm, dv, tm), BF16),
            jax.ShapeDtypeStruct((bsz, seq, d_inner), F32),
            jax.ShapeDtypeStruct((bsz, seq, d_conv), F32),
            jax.ShapeDtypeStruct((bsz, seq, 2 * d), F32),
            jax.ShapeDtypeStruct((bsz, seq, LANES), F32),
        ],
        compiler_params=_cparams(("arbitrary", "arbitrary")),
        name="in_proj",
    )(x, positions.reshape(bsz, seq, 1), mod3, norm_mix.reshape(1, d), freq, sign, dtb,
      b_gate.reshape(1, 2 * d), wq, wk, wv, wz, wx, wg, wdt)


def _attn_kernel(qt_ref, qtn_ref, k_ref, vt_ref, lam_ref, gain_ref, o_ref, acc_ref, m_ref, l_ref, qm_ref, sa_ref,
                 sb_ref, off_ref, *, tq, heads, lambda_init):
    i = pl.program_id(2)
    tk = 2 * tq
    row = lax.broadcasted_iota(I32, (LANES, tq), 0)

    def build_qm(src_ref, slot):
        for h in range(heads):
            qt = src_ref[0, 0, h * LANES:(h + 1) * LANES, :]
            zero = jnp.zeros_like(qt)
            qm_ref[slot, h] = jnp.concatenate([jnp.where(row < ATTN_QK_DIM, qt, zero),
                                               jnp.where(row >= ATTN_QK_DIM, qt, zero)], axis=1)

    def scores_into(pair, dst_ref, hs, qslot):
        koff = pl.multiple_of(pair * tk, tk)
        for h in hs:
            dst_ref[h] = _dot(k_ref[0, pl.ds(koff, tk), h * LANES:(h + 1) * LANES], qm_ref[qslot, h])

    all_heads = tuple(range(heads))

    @pl.when(i == 0)
    def _():
        build_qm(qt_ref, 0)
        scores_into(0, sa_ref, all_heads, 0)
        off_ref[0] = 0

    acc_ref[...] = jnp.zeros_like(acc_ref)
    m_ref[...] = jnp.full(m_ref.shape, NEG_BIG, F32)
    l_ref[...] = jnp.zeros_like(l_ref)
    qslot = i % 2
    off = off_ref[0]

    def softmax_pv(pair, src_ref, masked, hs, nblk=2):
        for h in hs:
            s = src_ref[h, :nblk * tq, :]
            if masked:
                kidx = pair * tk + lax.broadcasted_iota(I32, s.shape, 0)
                qidx = i * tq + lax.broadcasted_iota(I32, s.shape, 1) % tq
                s = jnp.where(kidx <= qidx, s, NEG_BIG)
            m_old = m_ref[h]
            m_new = jnp.maximum(m_old, jnp.max(s, axis=0, keepdims=True))
            alpha = jnp.exp2(m_old - m_new)
            p = jnp.exp2(s - m_new)
            l_ref[h] = alpha * l_ref[h] + jnp.sum(p, axis=0, keepdims=True)
            m_ref[h] = m_new
            pb = p.astype(BF16)
            pv = _dot(vt_ref[0, 2 * pair, h * LANES:(h + 1) * LANES, :], pb[:tq])
            if nblk == 2:
                pv = pv + _dot(vt_ref[0, 2 * pair + 1, h * LANES:(h + 1) * LANES, :], pb[tq:])
            acc_ref[h] = acc_ref[h] * alpha + pv

    bufs = (sa_ref, sb_ref)
    n_full = i // 2

    def body(n, carry):
        for par in range(2):
            @pl.when((n + off) % 2 == par)
            def _():
                for h in all_heads:
                    scores_into(n + 1, bufs[1 - par], (h,), qslot)
                    softmax_pv(n, bufs[par], False, (h,))
        return carry

    lax.fori_loop(0, n_full, body, 0)
    for par in range(2):
        for odd in range(2):
            @pl.when(((n_full + off) % 2 == par) & (i % 2 == odd))
            def _():
                build_qm(qtn_ref, 1 - odd)
                for h in all_heads:
                    scores_into(0, bufs[1 - par], (h,), 1 - odd)
                    softmax_pv(n_full, bufs[par], True, (h,), nblk=1 + odd)
                off_ref[0] = 1 - par

    lv = lam_ref[...]
    lam = (jnp.exp(jnp.sum(lv[0:1] * lv[1:2], axis=1, keepdims=True))
           - jnp.exp(jnp.sum(lv[2:3] * lv[3:4], axis=1, keepdims=True)) + lambda_init)
    for h in range(heads):
        acc = acc_ref[h]
        l = l_ref[h]
        o = acc[:, :tq] / l[:, :tq] - lam * (acc[:, tq:] / l[:, tq:])
        ms = jnp.mean(o * o, axis=0, keepdims=True)
        o = o * lax.rsqrt(ms + NORM_EPS) * gain_ref[...] * (1.0 - lambda_init)
        o_ref[0, :, h * LANES:(h + 1) * LANES] = o.T.astype(o_ref.dtype)


def _attn_call(qt4, k, vt4, lam_vecs, head_gain, lambda_init, heads):
    bsz, nq, dq, tq = qt4.shape
    seq = k.shape[1]
    dv = vt4.shape[2]
    w = heads * LANES
    assert nq % 2 == 0, "kv blocks are consumed in pairs"
    return pl.pallas_call(
        functools.partial(_attn_kernel, tq=tq, heads=heads, lambda_init=lambda_init),
        grid=(bsz, ATTN_HEADS // heads, nq),
        in_specs=[
            pl.BlockSpec((1, 1, w, tq), lambda b, h, i: (b, i, h, 0)),
            pl.BlockSpec((1, 1, w, tq), lambda b, h, i: (b, jnp.minimum(i + 1, nq - 1), h, 0)),
            pl.BlockSpec((1, seq, w), lambda b, h, i: (b, 0, h)),
            pl.BlockSpec((1, nq, w, tq), lambda b, h, i: (b, 0, h, 0)),
            pl.BlockSpec((4, ATTN_QK_DIM), lambda b, h, i: (0, 0)),
            pl.BlockSpec((ATTN_V_DIM, 1), lambda b, h, i: (0, 0)),
        ],
        out_specs=pl.BlockSpec((1, tq, w), lambda b, h, i: (b, i, h)),
        out_shape=jax.ShapeDtypeStruct((bsz, seq, dv), BF16),
        scratch_shapes=[pltpu.VMEM((heads, ATTN_V_DIM, 2 * tq), F32), pltpu.VMEM((heads, 1, 2 * tq), F32),
                        pltpu.VMEM((heads, 1, 2 * tq), F32), pltpu.VMEM((2, heads, LANES, 2 * tq), BF16),
                        pltpu.VMEM((heads, 2 * tq, 2 * tq), F32), pltpu.VMEM((heads, 2 * tq, 2 * tq), F32),
                        pltpu.SMEM((1,), I32)],
        compiler_params=_cparams(("arbitrary", "arbitrary", "arbitrary")),
        name="diff_attn",
    )(qt4, qt4, k, vt4, lam_vecs, head_gain.reshape(ATTN_V_DIM, 1))


def _ssd_kernel(xbc_ref, dt_ref, z_ref, cw_ref, cb_ref, alog_ref, dskip_ref, nw_ref, expand_ref,
                y_ref, tail_ref, state_ref, *, n_heads):
    c = pl.program_id(1)
    L = xbc_ref.shape[1]
    d_inner = n_heads * SSD_HEAD_DIM
    gw = d_inner // SSD_GROUPS
    hpg = n_heads // SSD_GROUPS

    @pl.when(c == 0)
    def _():
        tail_ref[...] = jnp.zeros_like(tail_ref)
        state_ref[...] = jnp.zeros_like(state_ref)

    u = xbc_ref[0]
    tail = tail_ref[...]
    row8 = lax.broadcasted_iota(I32, tail.shape, 0)
    conv = u * cw_ref[SSD_CONV - 1:SSD_CONV, :] + cb_ref[...]
    for s in range(1, SSD_CONV):
        r = pltpu.roll(u, s, axis=0)
        top = jnp.where(row8 < s, pltpu.roll(tail, s, axis=0), r[:SUBLANES])
        shifted = jnp.concatenate([top, r[SUBLANES:]], axis=0)
        conv = conv + shifted * cw_ref[SSD_CONV - 1 - s:SSD_CONV - s, :]
    tail_ref[...] = u[L - SUBLANES:, :]
    xbc = _silu(conv)
    xs = xbc[:, :d_inner]
    bmat = xbc[:, d_inner:d_inner + SSD_GROUPS * SSD_STATE]
    cmat = xbc[:, d_inner + SSD_GROUPS * SSD_STATE:]

    dt = dt_ref[0]
    a = dt * (-jnp.exp(alog_ref[...]))
    ri = lax.broadcasted_iota(I32, (L, L), 0)
    ci = lax.broadcasted_iota(I32, (L, L), 1)
    causal = ri >= ci
    tril = jnp.where(causal, 1.0, 0.0).astype(BF16)
    a_hi, a_mid, a_lo = _split3(a)
    a_cum = _dot(tril, a_hi) + _dot(tril, a_mid) + _dot(tril, a_lo)
    a_cum_t = a_cum.T
    a_last = a_cum[L - 1:L, :]
    ea = jnp.exp(a_cum)
    to_end = jnp.exp(a_last - a_cum)

    expand = expand_ref[...]

    def widen(v):
        hi, lo = _split2(v)
        return _dot(hi, expand) + _dot(lo, expand)

    dt_w = widen(dt)
    ea_w = widen(ea)
    te_w = widen(to_end)
    xdt = xs * dt_w
    xdt_b = xdt.astype(BF16)
    xte_b = (xdt * te_w).astype(BF16)

    y_parts = []
    for g in range(SSD_GROUPS):
        cg = cmat[:, g * SSD_STATE:(g + 1) * SSD_STATE]
        bg = bmat[:, g * SSD_STATE:(g + 1) * SSD_STATE]
        cg_b = cg.astype(BF16)
        cb = _dot_nt(cg_b, bg.astype(BF16))
        st = state_ref[g]
        y_off = _dot(cg_b, st.astype(BF16)) * ea_w[:, g * gw:(g + 1) * gw]
        y_diag = []
        for e in range(hpg):
            hh = g * hpg + e
            seg = a_cum[:, hh:hh + 1] - a_cum_t[hh:hh + 1, :]
            decay = jnp.exp(jnp.where(causal, seg, -jnp.inf))
            m = (cb * decay).astype(BF16)
            y_diag.append(_dot(m, xdt_b[:, hh * SSD_HEAD_DIM:(hh + 1) * SSD_HEAD_DIM]))
        y_parts.append(jnp.concatenate(y_diag, axis=1) + y_off)
        bt = bg.T.astype(BF16)
        state_ref[g] = st * ea_w[L - 1:L, g * gw:(g + 1) * gw] + _dot(bt, xte_b[:, g * gw:(g + 1) * gw])

    y = jnp.concatenate(y_parts, axis=1) + dskip_ref[...] * xs
    y = y * _silu(z_ref[0])
    outs = []
    for g in range(SSD_GROUPS):
        yg = y[:, g * gw:(g + 1) * gw]
        ms = jnp.mean(yg * yg, axis=-1, keepdims=True)
        outs.append(yg * lax.rsqrt(ms + NORM_EPS) * nw_ref[:, g * gw:(g + 1) * gw])
    y_ref[0] = jnp.concatenate(outs, axis=1).astype(y_ref.dtype)


def _ssd_call(xbc, dt, z, conv_w, conv_b, a_log, d_skip, ssd_norm):
    bsz, seq, d_conv = xbc.shape
    n_heads = a_log.shape[0]
    d_inner = n_heads * SSD_HEAD_DIM
    L = SSD_CHUNK
    alog = jnp.pad(a_log.astype(F32), (0, LANES - n_heads)).reshape(1, LANES)
    dskip = jnp.repeat(d_skip.astype(F32), SSD_HEAD_DIM).reshape(1, d_inner)
    head_of_lane = jnp.arange(d_inner) // SSD_HEAD_DIM
    expand = (jnp.arange(LANES)[:, None] == head_of_lane[None, :]).astype(BF16)
    const2 = lambda b, c: (0, 0)
    return pl.pallas_call(
        functools.partial(_ssd_kernel, n_heads=n_heads),
        grid=(bsz, seq // L),
        in_specs=[
            pl.BlockSpec((1, L, d_conv), lambda b, c: (b, c, 0)),
            pl.BlockSpec((1, L, LANES), lambda b, c: (b, c, 0)),
            pl.BlockSpec((1, L, d_inner), lambda b, c: (b, c, 0)),
            pl.BlockSpec((SSD_CONV, d_conv), const2),
            pl.BlockSpec((1, d_conv), const2),
            pl.BlockSpec((1, LANES), const2),
            pl.BlockSpec((1, d_inner), const2),
            pl.BlockSpec((1, d_inner), const2),
            pl.BlockSpec((LANES, d_inner), const2),
        ],
        out_specs=pl.BlockSpec((1, L, d_inner), lambda b, c: (b, c, 0)),
        out_shape=jax.ShapeDtypeStruct((bsz, seq, d_inner), BF16),
        scratch_shapes=[
            pltpu.VMEM((SUBLANES, d_conv), F32),
            pltpu.VMEM((SSD_GROUPS, SSD_STATE, d_inner // SSD_GROUPS), F32),
        ],
        compiler_params=_cparams(("arbitrary", "arbitrary")),
        name="ssd_scan",
    )(xbc, dt, z, conv_w, conv_b.reshape(1, d_conv), alog, dskip, ssd_norm.reshape(1, d_inner), expand)


def _merge_kernel(x_ref, ya_ref, ys_ref, gates_ref, mod_ref, nf_ref,
                  wba_ref, wbs_ref, wo_ref, wr_ref, wsg_ref, wsu_ref, wsd_ref,
                  h2_ref, base_ref, sc_ref):
    d = x_ref.shape[2]
    gate_m = mod_ref[0, 2:3, :]
    shift_f = mod_ref[0, 3:4, :]
    scale_f = mod_ref[0, 4:5, :]
    gate_f = mod_ref[0, 5:6, :]
    gates = gates_ref[0]
    mixed = gates[:, :d] * _dot(ya_ref[0], wba_ref[...]) + gates[:, d:] * _dot(ys_ref[0], wbs_ref[...])
    x1 = x_ref[0] + gate_m * _dot(mixed.astype(BF16), wo_ref[...])
    ms = jnp.mean(x1 * x1, axis=-1, keepdims=True)
    h2 = (x1 * lax.rsqrt(ms + NORM_EPS) * nf_ref[...]) * (1.0 + scale_f) + shift_f
    h2b = h2.astype(BF16)
    h2_ref[0] = _to_tiles(h2b)
    sc_ref[0] = _sigmoid(_dot(h2b, wr_ref[...]))
    act = _silu(_dot(h2b, wsg_ref[...])) * _dot(h2b, wsu_ref[...])
    shared = _dot(act.astype(BF16), wsd_ref[...])
    base_ref[0] = x1 + gate_f * shared


def _merge_call(x, y_attn, y_ssd, gates, mod3, norm_ffn, w_ba, w_bs, w_out, w_router, w_sg, w_su, w_sd, tm):
    bsz, seq, d = x.shape
    ws = [w.astype(BF16) for w in (w_ba, w_bs, w_out, w_router, w_sg, w_su, w_sd)]
    const2 = lambda b, i: (0, 0)
    wspec = lambda w: pl.BlockSpec(w.shape, const2, pipeline_mode=pl.Buffered(1))
    row_spec = lambda n: pl.BlockSpec((1, tm, n), lambda b, i: (b, i, 0))
    return pl.pallas_call(
        _merge_kernel,
        grid=(bsz, seq // tm),
        in_specs=[row_spec(d), row_spec(y_attn.shape[2]), row_spec(y_ssd.shape[2]), row_spec(2 * d),
                  pl.BlockSpec((1, 6, d), lambda b, i: (b, 0, 0)), pl.BlockSpec((1, d), const2)]
                 + [wspec(w) for w in ws],
        out_specs=[pl.BlockSpec((1, tm, d // LANES, LANES), lambda b, i: (b, i, 0, 0)), row_spec(d),
                   row_spec(N_EXPERTS)],
        out_shape=[jax.ShapeDtypeStruct((bsz, seq, d // LANES, LANES), BF16),
                   jax.ShapeDtypeStruct((bsz, seq, d), F32), jax.ShapeDtypeStruct((bsz, seq, N_EXPERTS), F32)],
        compiler_params=_cparams(("arbitrary", "arbitrary")),
        name="merge_ffn_pre",
    )(x, y_attn, y_ssd, gates, mod3, norm_ffn.reshape(1, d), *ws)


def _first_max(v, iota, sentinel):
    mx = jnp.max(v, axis=0, keepdims=True)
    first = jnp.min(jnp.where(v == mx, iota, sentinel), axis=0, keepdims=True)
    return mx, first


def _route_kernel(sc_ref, bias_ref, idx_ref, w_ref):
    sc = sc_ref[...].T
    tm = sc.shape[1]
    choice = sc + bias_ref[...]
    per_group = N_EXPERTS // N_EXPERT_GROUPS
    gi = lax.broadcasted_iota(I32, (per_group, tm), 0)
    gscore = []
    for g in range(N_EXPERT_GROUPS):
        cg = choice[g * per_group:(g + 1) * per_group, :]
        m1, f1 = _first_max(cg, gi, per_group)
        m2 = jnp.max(jnp.where(gi == f1, -jnp.inf, cg), axis=0, keepdims=True)
        gscore.append(m1 + m2)
    cur = jnp.concatenate(gscore, axis=0)
    giota = lax.broadcasted_iota(I32, cur.shape, 0)
    keep = jnp.zeros(cur.shape, F32)
    for _ in range(TOPK_GROUPS):
        _, f = _first_max(cur, giota, N_EXPERT_GROUPS)
        sel = giota == f
        keep = jnp.where(sel, 1.0, keep)
        cur = jnp.where(sel, -jnp.inf, cur)
    masked = jnp.concatenate(
        [jnp.where(keep[g:g + 1, :] > 0.5, choice[g * per_group:(g + 1) * per_group, :], -jnp.inf)
         for g in range(N_EXPERT_GROUPS)], axis=0)
    eiota = lax.broadcasted_iota(I32, masked.shape, 0)
    ids, ws = [], []
    for _ in range(TOP_K):
        _, f = _first_max(masked, eiota, N_EXPERTS)
        sel = eiota == f
        ids.append(f)
        ws.append(jnp.sum(jnp.where(sel, sc, 0.0), axis=0, keepdims=True))
        masked = jnp.where(sel, -jnp.inf, masked)
    idx_ref[...] = jnp.concatenate(ids, axis=0)
    w = jnp.concatenate(ws, axis=0)
    w = w / (jnp.sum(w, axis=0, keepdims=True) + 1e-20) * ROUTED_SCALE
    cols = [jnp.broadcast_to(w[k:k + 1, :], (LANES, tm)).T for k in range(TOP_K)]
    w_ref[...] = _to_tiles(jnp.concatenate(cols, axis=1))


def _route_call(scores, router_bias, tm):
    n = scores.shape[0]
    return pl.pallas_call(
        _route_kernel,
        grid=(n // tm,),
        in_specs=[pl.BlockSpec((tm, N_EXPERTS), lambda i: (i, 0)), pl.BlockSpec((N_EXPERTS, 1), lambda i: (0, 0))],
        out_specs=[pl.BlockSpec((TOP_K, tm), lambda i: (0, i)), pl.BlockSpec((tm, TOP_K, LANES), lambda i: (i, 0, 0))],
        out_shape=[jax.ShapeDtypeStruct((TOP_K, n), I32), jax.ShapeDtypeStruct((n, TOP_K, LANES), F32)],
        compiler_params=_cparams(("arbitrary",)),
        name="route_topk",
    )(scores, router_bias.astype(F32).reshape(N_EXPERTS, 1))


def _rank_kernel(idx_ref, dest_ref, cnt_ref, base_ref, *, blk):
    phase = pl.program_id(0)
    i = pl.program_id(1)
    idx = idx_ref[...]
    tm = idx.shape[1]
    eiota = lax.broadcasted_iota(I32, (N_EXPERTS, tm), 0)
    onehot = jnp.zeros((N_EXPERTS, tm), F32)
    for k in range(TOP_K):
        onehot = onehot + jnp.where(eiota == idx[k:k + 1, :], 1.0, 0.0)

    @pl.when((phase == 0) & (i == 0))
    def _():
        base_ref[...] = jnp.zeros_like(base_ref)

    @pl.when((phase == 1) & (i == 0))
    def _():
        counts = base_ref[...]
        cnt_ref[...] = counts
        padded = jnp.floor((counts + (blk - 1)) * (1.0 / blk)) * blk
        ri = lax.broadcasted_iota(I32, (N_EXPERTS, N_EXPERTS), 0)
        ci = lax.broadcasted_iota(I32, (N_EXPERTS, N_EXPERTS), 1)
        lower = jnp.where(ci < ri, 1.0, 0.0).astype(BF16)
        hi, mid, lo = _split3(padded)
        base_ref[...] = _dot(lower, hi) + _dot(lower, mid) + _dot(lower, lo)

    @pl.when(phase == 1)
    def _():
        ri = lax.broadcasted_iota(I32, (tm, tm), 0)
        ci = lax.broadcasted_iota(I32, (tm, tm), 1)
        before = jnp.where(ri < ci, 1.0, 0.0).astype(BF16)
        prior = _dot(onehot.astype(BF16), before) + base_ref[:, 0:1]
        rows = []
        for k in range(TOP_K):
            rows.append(jnp.sum(jnp.where(eiota == idx[k:k + 1, :], prior, 0.0), axis=0, keepdims=True))
        dest_ref[...] = jnp.concatenate(rows, axis=0).astype(I32)

    base_ref[...] = base_ref[...] + jnp.sum(onehot, axis=1, keepdims=True)


def _rank_call(idx, tm, blk):
    n = idx.shape[1]
    return pl.pallas_call(
        functools.partial(_rank_kernel, blk=blk),
        grid=(2, n // tm),
        in_specs=[pl.BlockSpec((TOP_K, tm), lambda p, i: (0, i))],
        out_specs=[pl.BlockSpec((TOP_K, tm), lambda p, i: (0, i * p)),
                   pl.BlockSpec((N_EXPERTS, LANES), lambda p, i: (0, 0))],
        out_shape=[jax.ShapeDtypeStruct((TOP_K, n), I32), jax.ShapeDtypeStruct((N_EXPERTS, LANES), F32)],
        scratch_shapes=[pltpu.VMEM((N_EXPERTS, LANES), F32)],
        compiler_params=_cparams(("arbitrary", "arbitrary")),
        name="expert_rank",
    )(idx)


def _dispatch_kernel(pend_ref, padded_ref, nu_ref, dest_ref, h_ref, xs_ref, zero_ref, sem_z, sem_s, *, blk):
    i = pl.program_id(0)
    tm = h_ref.shape[0]
    n_blocks = xs_ref.shape[0] // blk

    def zero_copy(start):
        return pltpu.make_async_copy(zero_ref, xs_ref.at[pl.ds(pl.multiple_of(start, blk), blk)], sem_z)

    @pl.when(i == 0)
    def _():
        zero_ref[...] = jnp.zeros_like(zero_ref)

        def start(e, _):
            @pl.when(padded_ref[e] > 0)
            def _():
                zero_copy(pend_ref[e] - blk).start()
            return 0

        def wait(e, _):
            @pl.when(padded_ref[e] > 0)
            def _():
                zero_copy(pend_ref[e] - blk).wait()
            return 0

        def start_idle(b, _):
            zero_copy(b * blk).start()
            return 0

        def wait_idle(b, _):
            zero_copy(b * blk).wait()
            return 0

        lax.fori_loop(0, N_EXPERTS, start, 0)
        lax.fori_loop(nu_ref[0], n_blocks, start_idle, 0)
        lax.fori_loop(0, N_EXPERTS, wait, 0)
        lax.fori_loop(nu_ref[0], n_blocks, wait_idle, 0)

    def issue(g, _):
        for u in range(ISSUE_UNROLL):
            t = g * ISSUE_UNROLL + u
            for k in range(TOP_K):
                pltpu.make_async_copy(h_ref.at[t], xs_ref.at[dest_ref[k, t]], sem_s).start(priority=k % 2)
        return 0

    lax.fori_loop(0, tm // ISSUE_UNROLL, issue, 0)
    for k in range(TOP_K):
        pltpu.make_async_copy(h_ref, xs_ref.at[pl.ds(0, tm)], sem_s).wait()


def _dispatch_call(pend, padded, n_used, dest, h2t, n_slots, blk, tm):
    n, s, lanes = h2t.shape
    return pl.pallas_call(
        functools.partial(_dispatch_kernel, blk=blk),
        grid_spec=pltpu.PrefetchScalarGridSpec(
            num_scalar_prefetch=3,
            grid=(n // tm,),
            in_specs=[
                pl.BlockSpec((TOP_K, tm), lambda i, pe, pa, nu: (0, i), memory_space=pltpu.SMEM),
                pl.BlockSpec((tm, s, lanes), lambda i, pe, pa, nu: (i, 0, 0)),
            ],
            out_specs=pl.BlockSpec(memory_space=pl.ANY),
            scratch_shapes=[pltpu.VMEM((blk, s, lanes), h2t.dtype), pltpu.SemaphoreType.DMA(()),
                            pltpu.SemaphoreType.DMA(())],
        ),
        out_shape=jax.ShapeDtypeStruct((n_slots, s, lanes), h2t.dtype),
        compiler_params=_cparams(("arbitrary",)),
        name="moe_dispatch",
    )(pend, padded, n_used, dest, h2t)


RING = 3


def _expert_kernel(be_ref, nu_ref, first_ref, wslot_ref, nxt1_ref, nxt2_ref, xs_hbm, wg_hbm, wu_hbm, wd_hbm, y_hbm,
                   xbuf, ybuf, zbuf, wg_f, wu_f, wd_f, wg_b, wu_b, wd_b, x_sems, y_sems, w_sems, *, blk):
    i = pl.program_id(0)
    nu = nu_ref[0]
    last = pl.num_programs(0) - 1

    def x_copy(b):
        slot = b % RING
        return pltpu.make_async_copy(xs_hbm.at[pl.ds(pl.multiple_of(b * blk, blk), blk)], xbuf.at[slot],
                                     x_sems.at[slot])

    def y_copy(b, src):
        return pltpu.make_async_copy(src, y_hbm.at[pl.ds(pl.multiple_of(b * blk, blk), blk)], y_sems.at[b % RING])

    def weight_copies(e, slot):
        return (pltpu.make_async_copy(wg_hbm.at[e], wg_f.at[slot], w_sems.at[slot]),
                pltpu.make_async_copy(wu_hbm.at[e], wu_f.at[slot], w_sems.at[slot]),
                pltpu.make_async_copy(wd_hbm.at[e], wd_f.at[slot], w_sems.at[slot]))

    @pl.when(i == 0)
    def _():
        zbuf[...] = jnp.zeros_like(zbuf)
        x_copy(0).start()

        @pl.when(nu > 1)
        def _():
            x_copy(1).start()

        for cp in weight_copies(be_ref[0], 0):
            cp.start()

        @pl.when(nxt1_ref[0] >= 0)
        def _():
            for cp in weight_copies(nxt1_ref[0], 1):
                cp.start()

    @pl.when(i < nu)
    def _():
        @pl.when(i + 2 < nu)
        def _():
            x_copy(i + 2).start()

        @pl.when(first_ref[i] == 1)
        def _():
            slot = wslot_ref[i]

            @pl.when(nxt2_ref[i] >= 0)
            def _():
                for cp in weight_copies(nxt2_ref[i], (slot + 2) % RING):
                    cp.start()

            for cp in weight_copies(be_ref[i], slot):
                cp.wait()
            wg_b[...] = wg_f[slot].astype(BF16)
            wu_b[...] = wu_f[slot].astype(BF16)
            wd_b[...] = wd_f[slot].astype(BF16)

        x_copy(i).wait()
        slot = i % RING
        half = blk // 2
        xs_h = [_from_tiles(xbuf[slot, r * half:(r + 1) * half]) for r in range(2)]
        gu = [(_dot(xh, wg_b[...]), _dot(xh, wu_b[...])) for xh in xs_h]
        for r in range(2):
            act = _silu(gu[r][0]) * gu[r][1]
            ybuf[slot, r * half:(r + 1) * half] = _to_tiles(_dot(act.astype(BF16), wd_b[...]).astype(ybuf.dtype))
        y_copy(i, ybuf.at[slot]).start()

    @pl.when(i >= nu)
    def _():
        y_copy(i, zbuf).start()

    @pl.when(i >= 2)
    def _():
        y_copy(i - 2, zbuf).wait()

    @pl.when(i == last)
    def _():
        @pl.when(i >= 1)
        def _():
            y_copy(i - 1, zbuf).wait()

        y_copy(i, zbuf).wait()


def _expert_call(block_e, n_used, first, wslot, nxt1, nxt2, xs, w_gate, w_up, w_down, blk):
    n_slots, s, lanes = xs.shape
    _, d, ff = w_gate.shape
    n_blocks = n_slots // blk
    hbm = pl.BlockSpec(memory_space=pl.ANY)
    return pl.pallas_call(
        functools.partial(_expert_kernel, blk=blk),
        grid_spec=pltpu.PrefetchScalarGridSpec(
            num_scalar_prefetch=6,
            grid=(n_blocks,),
            in_specs=[hbm, hbm, hbm, hbm],
            out_specs=hbm,
            scratch_shapes=[pltpu.VMEM((RING, blk, s, lanes), xs.dtype), pltpu.VMEM((RING, blk, s, lanes), BF16),
                            pltpu.VMEM((blk, s, lanes), BF16),
                            pltpu.VMEM((RING, d, ff), F32), pltpu.VMEM((RING, d, ff), F32),
                            pltpu.VMEM((RING, ff, d), F32),
                            pltpu.VMEM((d, ff), BF16), pltpu.VMEM((d, ff), BF16), pltpu.VMEM((ff, d), BF16),
                            pltpu.SemaphoreType.DMA((RING,)), pltpu.SemaphoreType.DMA((RING,)),
                            pltpu.SemaphoreType.DMA((RING,))],
        ),
        out_shape=jax.ShapeDtypeStruct((n_slots, s, lanes), BF16),
        compiler_params=_cparams(("arbitrary",)),
        name="moe_experts",
    )(block_e, n_used, first, wslot, nxt1, nxt2, xs, w_gate, w_up, w_down)


def _combine_kernel(dest_ref, dnext_ref, w_ref, base_ref, gf_ref, nfin_ref, y_ref, o_ref, buf_ref, sems, *,
                    final_norm):
    i = pl.program_id(0)
    last = pl.num_programs(0) - 1
    tc = base_ref.shape[0]

    def issue_all(d_ref, slot):
        def issue(g, _):
            for u in range(ISSUE_UNROLL):
                t = g * ISSUE_UNROLL + u
                for k in range(TOP_K):
                    pltpu.make_async_copy(y_ref.at[d_ref[k, t]], buf_ref.at[slot, k, t],
                                          sems.at[slot]).start(priority=k % 2)
            return 0

        lax.fori_loop(0, tc // ISSUE_UNROLL, issue, 0)

    @pl.when(i == 0)
    def _():
        issue_all(dest_ref, 0)

    for par in range(2):
        @pl.when((i < last) & (i % 2 == par))
        def _():
            issue_all(dnext_ref, 1 - par)

    for par in range(2):
        @pl.when(i % 2 == par)
        def _():
            for k in range(TOP_K):
                pltpu.make_async_copy(y_ref.at[pl.ds(0, tc)], buf_ref.at[par, k], sems.at[par]).wait()

    slot = i % 2
    routed = buf_ref[slot, 0].astype(F32) * w_ref[:, 0:1, :]
    for k in range(1, TOP_K):
        routed = routed + buf_ref[slot, k].astype(F32) * w_ref[:, k:k + 1, :]
    x2 = base_ref[...] + gf_ref[0] * _from_tiles(routed)
    if final_norm:
        ms = jnp.mean(x2 * x2, axis=-1, keepdims=True)
        x2 = x2 * lax.rsqrt(ms + NORM_EPS) * nfin_ref[...]
    o_ref[...] = x2


def _combine_call(dest, w_tok, base, gate_f, norm_final, y, tokens_per_batch, tc, final_norm):
    n, d = base.shape
    steps_per_batch = tokens_per_batch // tc
    steps = n // tc
    return pl.pallas_call(
        functools.partial(_combine_kernel, final_norm=final_norm),
        grid=(steps,),
        in_specs=[
            pl.BlockSpec((TOP_K, tc), lambda i: (0, i), memory_space=pltpu.SMEM),
            pl.BlockSpec((TOP_K, tc), lambda i: (0, jnp.minimum(i + 1, steps - 1)), memory_space=pltpu.SMEM),
            pl.BlockSpec((tc, TOP_K, LANES), lambda i: (i, 0, 0)),
            pl.BlockSpec((tc, d), lambda i: (i, 0)),
            pl.BlockSpec((1, 1, d), lambda i: (i // steps_per_batch, 0, 0)),
            pl.BlockSpec((1, d), lambda i: (0, 0)),
            pl.BlockSpec(memory_space=pl.ANY),
        ],
        out_specs=pl.BlockSpec((tc, d), lambda i: (i, 0)),
        out_shape=jax.ShapeDtypeStruct((n, d), F32),
        scratch_shapes=[pltpu.VMEM((2, TOP_K, tc, d // LANES, LANES), y.dtype), pltpu.SemaphoreType.DMA((2,))],
        compiler_params=_cparams(("arbitrary",)),
        name="moe_combine",
    )(dest, dest, w_tok, base, gate_f, norm_final.reshape(1, d), y)


def _moe_blocks(n_pairs, blk):
    return -(-n_pairs // blk) + N_EXPERTS


def _moe_layout(counts, n_pairs, blk):
    n_blocks = _moe_blocks(n_pairs, blk)
    padded = (counts + blk - 1) // blk * blk
    pend = jnp.cumsum(padded)
    n_used = pend[-1] // blk
    blk_start = jnp.arange(n_blocks, dtype=I32) * blk
    block_e = jnp.sum((pend[None, :] <= blk_start[:, None]).astype(I32), axis=1)
    block_e = jnp.minimum(block_e, N_EXPERTS - 1)
    used = jnp.arange(n_blocks) < n_used
    prev_e = jnp.concatenate([jnp.full((1,), -1, I32), block_e[:-1]])
    first = (used & (block_e != prev_e)).astype(I32)
    has = padded > 0
    ids = jnp.arange(N_EXPERTS, dtype=I32)
    ordinal = jnp.cumsum(has.astype(I32)) - 1
    at_or_after = lax.cummin(jnp.where(has, ids, N_EXPERTS), axis=0, reverse=True)
    after = jnp.concatenate([at_or_after[1:], jnp.full((2,), N_EXPERTS, I32)])
    nxt1_e = after[:N_EXPERTS]
    nxt2_e = after[jnp.minimum(nxt1_e, N_EXPERTS)]
    fix = lambda v: jnp.where(v >= N_EXPERTS, -1, v).astype(I32)
    wslot = (ordinal % RING).astype(I32)[block_e]
    nxt1 = fix(nxt1_e)[block_e]
    nxt2 = fix(nxt2_e)[block_e]
    return (pend.astype(I32), padded.astype(I32), block_e.astype(I32), n_used.reshape(1).astype(I32),
            first, wslot, nxt1, nxt2, n_blocks)


def kernel(x, c, positions, w_ada, b_ada, norm_mix, w_in, b_gate, lambda_q1, lambda_k1, lambda_q2, lambda_k2,
           attn_head_norm, conv_w, conv_b, dt_bias, a_log, d_skip, ssd_norm, w_branch_attn, w_branch_ssd, w_out,
           norm_ffn, w_router, router_bias, w_exp_gate, w_exp_up, w_exp_down, w_sh_gate, w_sh_up, w_sh_down,
           norm_final):
    bsz, seq, d = x.shape
    n = bsz * seq
    depth = w_ada.shape[0]
    tm = min(TOKEN_TILE, seq)
    c_pad = jnp.pad(c, ((0, -bsz % SUBLANES), (0, 0)))
    for l in range(depth):
        last = l == depth - 1
        mod3 = _ada_call(c_pad, w_ada[l], b_ada[l])[:bsz].reshape(bsz, 6, d)
        qt4, k, vt4, z, xbc, gates, dt = _inproj_call(x, positions, mod3, norm_mix[l], b_gate[l], dt_bias[l],
                                                      w_in[l], tm)
        lambda_init = 0.8 - 0.6 * math.exp(-0.3 * l)
        lam_vecs = jnp.stack([lambda_q1[l], lambda_k1[l], lambda_q2[l], lambda_k2[l]]).astype(F32)
        y_attn = _attn_call(qt4, k, vt4, lam_vecs, attn_head_norm[l].astype(F32), lambda_init, ATTN_HEADS_PER_STEP)
        y_ssd = _ssd_call(xbc, dt, z, conv_w[l], conv_b[l], a_log[l], d_skip[l], ssd_norm[l])
        h2, base, scores = _merge_call(x, y_attn, y_ssd, gates, mod3, norm_ffn[l], w_branch_attn[l],
                                       w_branch_ssd[l], w_out[l], w_router[l], w_sh_gate[l], w_sh_up[l],
                                       w_sh_down[l], tm)
        rt = min(ROUTE_TILE, n)
        idx, w_tiles = _route_call(scores.reshape(n, N_EXPERTS), router_bias[l], rt)
        dest, cnt = _rank_call(idx, rt, MOE_BLOCK)
        pend, padded, block_e, n_used, first, wslot, nxt1, nxt2, n_blocks = _moe_layout(
            cnt[:, 0].astype(I32), n * TOP_K, MOE_BLOCK)
        xs = _dispatch_call(pend, padded, n_used, dest, h2.reshape(n, d // LANES, LANES), n_blocks * MOE_BLOCK,
                            MOE_BLOCK, tm)
        y = _expert_call(block_e, n_used, first, wslot, nxt1, nxt2, xs, w_exp_gate[l], w_exp_up[l], w_exp_down[l],
                         MOE_BLOCK)
        out = _combine_call(dest, w_tiles, base.reshape(n, d), mod3[:, 5:6, :], norm_final, y, seq,
                            min(COMBINE_TILE, seq), final_norm=last)
        x = out.reshape(bsz, seq, d)
    return x
```

```python
import functools
import math

import jax
import jax.numpy as jnp
from jax import lax
from jax.experimental import pallas as pl
from jax.experimental.pallas import tpu as pltpu

F32 = jnp.float32
BF16 = jnp.bfloat16
I32 = jnp.int32

ATTN_HEADS = 8
ATTN_QK_DIM = 64
ATTN_V_DIM = 128
ROPE_DIM = 16
ROPE_THETA = 500000.0
SSD_HEAD_DIM = 64
SSD_GROUPS = 4
SSD_STATE = 128
SSD_CONV = 4
SSD_CHUNK = 128
N_EXPERTS = 256
TOP_K = 8
N_EXPERT_GROUPS = 8
TOPK_GROUPS = 4
ROUTED_SCALE = 2.5
NORM_EPS = 1e-6
LOG2E = 1.4426950408889634
NEG_BIG = -1e30

LANES = 128
SUBLANES = 8
VMEM_LIMIT_BYTES = 56 * 1024 * 1024

TOKEN_TILE = 256
MOE_BLOCK = 256
ROUTE_TILE = 1024
DISPATCH_TILE = 512
COMBINE_TILE = 256
ATTN_HEADS_PER_STEP = 4
ISSUE_UNROLL = 4


def _cparams(sem, vmem=VMEM_LIMIT_BYTES):
    return pltpu.CompilerParams(dimension_semantics=sem, vmem_limit_bytes=vmem)


def _dot(a, b):
    return jnp.dot(a, b, preferred_element_type=F32)


def _dot_nt(a, b):
    return lax.dot_general(a, b, (((1,), (1,)), ((), ())), preferred_element_type=F32)


def _sigmoid(x):
    return 1.0 / (1.0 + jnp.exp(-x))


def _silu(x):
    return x * _sigmoid(x)


def _to_tiles(x):
    return x.reshape(x.shape[0], x.shape[1] // LANES, LANES)


def _from_tiles(x):
    return x.reshape(x.shape[0], x.shape[1] * x.shape[2])


def _split2(x):
    hi = x.astype(BF16)
    lo = (x - hi.astype(F32)).astype(BF16)
    return hi, lo


def _split3(x):
    hi = x.astype(BF16)
    r = x - hi.astype(F32)
    mid = r.astype(BF16)
    lo = (r - mid.astype(F32)).astype(BF16)
    return hi, mid, lo


def _ada_kernel(c_ref, w_ref, b_ref, o_ref):
    o_ref[...] = _dot(_silu(c_ref[...]).astype(BF16), w_ref[...].astype(BF16)) + b_ref[...]


def _ada_call(c_pad, w_ada, b_ada):
    rows, d = c_pad.shape
    n = w_ada.shape[1]
    tn = 1024
    return pl.pallas_call(
        _ada_kernel,
        grid=(n // tn,),
        in_specs=[
            pl.BlockSpec((rows, d), lambda j: (0, 0)),
            pl.BlockSpec((d, tn), lambda j: (0, j)),
            pl.BlockSpec((1, tn), lambda j: (0, j)),
        ],
        out_specs=pl.BlockSpec((rows, tn), lambda j: (0, j)),
        out_shape=jax.ShapeDtypeStruct((rows, n), F32),
        compiler_params=_cparams(("arbitrary",)),
        name="ada_mod",
    )(c_pad, w_ada, b_ada.reshape(1, n))


def _inproj_kernel(x_ref, pos_ref, mod_ref, nm_ref, freq_ref, sign_ref, dtb_ref, bg_ref,
                   wq_ref, wk_ref, wv_ref, wz_ref, wx_ref, wg_ref, wdt_ref,
                   qt_ref, k_ref, vt_ref, z_ref, xbc_ref, gates_ref, dt_ref, *, q_scale):
    x = x_ref[0]
    ms = jnp.mean(x * x, axis=-1, keepdims=True)
    shift = mod_ref[0, 0:1, :]
    scale = mod_ref[0, 1:2, :]
    h = (x * lax.rsqrt(ms + NORM_EPS) * nm_ref[...]) * (1.0 + scale) + shift
    hb = h.astype(BF16)

    ang = pos_ref[0].astype(F32) * freq_ref[...]
    cos = jnp.cos(ang)
    sin_signed = jnp.sin(ang) * sign_ref[...]
    lane = lax.broadcasted_iota(I32, ang.shape, 1) % ATTN_QK_DIM
    first_half = lane < (ROPE_DIM // 2)

    def rope(t):
        outs = []
        for hd in range(t.shape[1] // LANES):
            th = t[:, hd * LANES:(hd + 1) * LANES]
            up = pltpu.roll(th, LANES - ROPE_DIM // 2, axis=1)
            down = pltpu.roll(th, ROPE_DIM // 2, axis=1)
            partner = jnp.where(first_half, up, down)
            outs.append(th * cos + partner * sin_signed)
        return jnp.concatenate(outs, axis=1)

    q = rope(_dot(hb, wq_ref[...])) * q_scale
    qt_ref[0, 0] = q.T.astype(BF16)
    k = rope(_dot(hb, wk_ref[...]))
    k_ref[0] = k.astype(BF16)
    v = _dot(hb, wv_ref[...])
    vt_ref[0, 0] = v.T.astype(BF16)
    z_ref[0] = _dot(hb, wz_ref[...])
    xbc_ref[0] = _dot(hb, wx_ref[...])
    g = _dot(hb, wg_ref[...]) + bg_ref[...]
    gates_ref[0] = _sigmoid(g)
    dtr = _dot(hb, wdt_ref[...]) + dtb_ref[...]
    dt_ref[0] = jnp.maximum(dtr, 0.0) + jnp.log1p(jnp.exp(-jnp.abs(dtr)))


def _inproj_call(x, positions, mod3, norm_mix, b_gate, dt_bias, w_in, tm):
    bsz, seq, d = x.shape
    dq = ATTN_HEADS * 2 * ATTN_QK_DIM
    dv = ATTN_HEADS * ATTN_V_DIM
    n_heads_ssd = dt_bias.shape[0]
    d_inner = n_heads_ssd * SSD_HEAD_DIM
    d_conv = d_inner + 2 * SSD_GROUPS * SSD_STATE
    sizes = (dq, dq, dv, d_inner, d_conv, n_heads_ssd, 2 * d)
    offs = [0]
    for s in sizes:
        offs.append(offs[-1] + s)
    wb = w_in.astype(BF16)
    wq, wk, wv, wz, wx, wdt, wg = (wb[:, offs[i]:offs[i + 1]] for i in range(7))
    wdt = jnp.pad(wdt, ((0, 0), (0, LANES - n_heads_ssd)))
    dtb = jnp.pad(dt_bias.astype(F32), (0, LANES - n_heads_ssd)).reshape(1, LANES)

    half = ROPE_DIM // 2
    lane_d = jnp.arange(LANES) % ATTN_QK_DIM
    inv_freq = 1.0 / (ROPE_THETA ** (jnp.arange(0, ROPE_DIM, 2, dtype=F32) / ROPE_DIM))
    freq = jnp.where(lane_d < ROPE_DIM, inv_freq[lane_d % half], 0.0).reshape(1, LANES).astype(F32)
    sign = jnp.where(lane_d < half, -1.0, jnp.where(lane_d < ROPE_DIM, 1.0, 0.0)).reshape(1, LANES).astype(F32)

    const2 = lambda b, i: (0, 0)
    wspec = lambda w: pl.BlockSpec(w.shape, const2, pipeline_mode=pl.Buffered(1))
    row_spec = lambda n: pl.BlockSpec((1, tm, n), lambda b, i: (b, i, 0))
    col_spec = lambda n: pl.BlockSpec((1, 1, n, tm), lambda b, i: (b, i, 0, 0))
    q_scale = (ATTN_QK_DIM ** -0.5) * LOG2E
    return pl.pallas_call(
        functools.partial(_inproj_kernel, q_scale=q_scale),
        grid=(bsz, seq // tm),
        in_specs=[
            row_spec(d),
            pl.BlockSpec((1, tm, 1), lambda b, i: (b, i, 0)),
            pl.BlockSpec((1, 6, d), lambda b, i: (b, 0, 0)),
            pl.BlockSpec((1, d), const2),
            pl.BlockSpec((1, LANES), const2),
            pl.BlockSpec((1, LANES), const2),
            pl.BlockSpec((1, LANES), const2),
            pl.BlockSpec((1, 2 * d), const2),
            wspec(wq), wspec(wk), wspec(wv), wspec(wz), wspec(wx), wspec(wg), wspec(wdt),
        ],
        out_specs=[col_spec(dq), row_spec(dq), col_spec(dv), row_spec(d_inner), row_spec(d_conv),
                   row_spec(2 * d), row_spec(LANES)],
        out_shape=[
            jax.ShapeDtypeStruct((bsz, seq // tm, dq, tm), BF16),
            jax.ShapeDtypeStruct((bsz, seq, dq), BF16),
            jax.ShapeDtypeStruct((bsz, seq // tm, dv, tm), BF16),
            jax.ShapeDtypeStruct((bsz, seq, d_inner), F32),
            jax.ShapeDtypeStruct((bsz, seq, d_conv), F32),
            jax.ShapeDtypeStruct((bsz, seq, 2 * d), F32),
            jax.ShapeDtypeStruct((bsz, seq, LANES), F32),
        ],
        compiler_params=_cparams(("arbitrary", "arbitrary")),
        name="in_proj",
    )(x, positions.reshape(bsz, seq, 1), mod3, norm_mix.reshape(1, d), freq, sign, dtb,
      b_gate.reshape(1, 2 * d), wq, wk, wv, wz, wx, wg, wdt)


def _attn_kernel(qt_ref, qtn_ref, k_ref, vt_ref, lam_ref, gain_ref, o_ref, acc_ref, m_ref, l_ref, qm_ref, sa_ref,
                 sb_ref, off_ref, *, tq, heads, lambda_init):
    i = pl.program_id(2)
    tk = 2 * tq
    row = lax.broadcasted_iota(I32, (LANES, tq), 0)

    def build_qm(src_ref, slot):
        for h in range(heads):
            qt = src_ref[0, 0, h * LANES:(h + 1) * LANES, :]
            zero = jnp.zeros_like(qt)
            qm_ref[slot, h] = jnp.concatenate([jnp.where(row < ATTN_QK_DIM, qt, zero),
                                               jnp.where(row >= ATTN_QK_DIM, qt, zero)], axis=1)

    def scores_into(pair, dst_ref, hs, qslot):
        koff = pl.multiple_of(pair * tk, tk)
        for h in hs:
            dst_ref[h] = _dot(k_ref[0, pl.ds(koff, tk), h * LANES:(h + 1) * LANES], qm_ref[qslot, h])

    all_heads = tuple(range(heads))

    @pl.when(i == 0)
    def _():
        build_qm(qt_ref, 0)
        scores_into(0, sa_ref, all_heads, 0)
        off_ref[0] = 0

    acc_ref[...] = jnp.zeros_like(acc_ref)
    m_ref[...] = jnp.full(m_ref.shape, NEG_BIG, F32)
    l_ref[...] = jnp.zeros_like(l_ref)
    qslot = i % 2
    off = off_ref[0]

    def softmax_pv(pair, src_ref, masked, hs, nblk=2):
        for h in hs:
            s = src_ref[h, :nblk * tq, :]
            if masked:
                kidx = pair * tk + lax.broadcasted_iota(I32, s.shape, 0)
                qidx = i * tq + lax.broadcasted_iota(I32, s.shape, 1) % tq
                s = jnp.where(kidx <= qidx, s, NEG_BIG)
            m_old = m_ref[h]
            m_new = jnp.maximum(m_old, jnp.max(s, axis=0, keepdims=True))
            alpha = jnp.exp2(m_old - m_new)
            p = jnp.exp2(s - m_new)
            l_ref[h] = alpha * l_ref[h] + jnp.sum(p, axis=0, keepdims=True)
            m_ref[h] = m_new
            pb = p.astype(BF16)
            pv = _dot(vt_ref[0, 2 * pair, h * LANES:(h + 1) * LANES, :], pb[:tq])
            if nblk == 2:
                pv = pv + _dot(vt_ref[0, 2 * pair + 1, h * LANES:(h + 1) * LANES, :], pb[tq:])
            acc_ref[h] = acc_ref[h] * alpha + pv

    bufs = (sa_ref, sb_ref)
    n_full = i // 2

    def body(n, carry):
        for par in range(2):
            @pl.when((n + off) % 2 == par)
            def _():
                for h in all_heads:
                    scores_into(n + 1, bufs[1 - par], (h,), qslot)
                    softmax_pv(n, bufs[par], False, (h,))
        return carry

    lax.fori_loop(0, n_full, body, 0)
    for par in range(2):
        for odd in range(2):
            @pl.when(((n_full + off) % 2 == par) & (i % 2 == odd))
            def _():
                build_qm(qtn_ref, 1 - odd)
                for h in all_heads:
                    scores_into(0, bufs[1 - par], (h,), 1 - odd)
                    softmax_pv(n_full, bufs[par], True, (h,), nblk=1 + odd)
                off_ref[0] = 1 - par

    lv = lam_ref[...]
    lam = (jnp.exp(jnp.sum(lv[0:1] * lv[1:2], axis=1, keepdims=True))
           - jnp.exp(jnp.sum(lv[2:3] * lv[3:4], axis=1, keepdims=True)) + lambda_init)
    for h in range(heads):
        acc = acc_ref[h]
        l = l_ref[h]
        o = acc[:, :tq] / l[:, :tq] - lam * (acc[:, tq:] / l[:, tq:])
        ms = jnp.mean(o * o, axis=0, keepdims=True)
        o = o * lax.rsqrt(ms + NORM_EPS) * gain_ref[...] * (1.0 - lambda_init)
        o_ref[0, :, h * LANES:(h + 1) * LANES] = o.T.astype(o_ref.dtype)


def _attn_call(qt4, k, vt4, lam_vecs, head_gain, lambda_init, heads):
    bsz, nq, dq, tq = qt4.shape
    seq = k.shape[1]
    dv = vt4.shape[2]
    w = heads * LANES
    assert nq % 2 == 0, "kv blocks are consumed in pairs"
    return pl.pallas_call(
        functools.partial(_attn_kernel, tq=tq, heads=heads, lambda_init=lambda_init),
        grid=(bsz, ATTN_HEADS // heads, nq),
        in_specs=[
            pl.BlockSpec((1, 1, w, tq), lambda b, h, i: (b, i, h, 0)),
            pl.BlockSpec((1, 1, w, tq), lambda b, h, i: (b, jnp.minimum(i + 1, nq - 1), h, 0)),
            pl.BlockSpec((1, seq, w), lambda b, h, i: (b, 0, h)),
            pl.BlockSpec((1, nq, w, tq), lambda b, h, i: (b, 0, h, 0)),
            pl.BlockSpec((4, ATTN_QK_DIM), lambda b, h, i: (0, 0)),
            pl.BlockSpec((ATTN_V_DIM, 1), lambda b, h, i: (0, 0)),
        ],
        out_specs=pl.BlockSpec((1, tq, w), lambda b, h, i: (b, i, h)),
        out_shape=jax.ShapeDtypeStruct((bsz, seq, dv), BF16),
        scratch_shapes=[pltpu.VMEM((heads, ATTN_V_DIM, 2 * tq), F32), pltpu.VMEM((heads, 1, 2 * tq), F32),
                        pltpu.VMEM((heads, 1, 2 * tq), F32), pltpu.VMEM((2, heads, LANES, 2 * tq), BF16),
                        pltpu.VMEM((heads, 2 * tq, 2 * tq), F32), pltpu.VMEM((heads, 2 * tq, 2 * tq), F32),
                        pltpu.SMEM((1,), I32)],
        compiler_params=_cparams(("arbitrary", "arbitrary", "arbitrary")),
        name="diff_attn",
    )(qt4, qt4, k, vt4, lam_vecs, head_gain.reshape(ATTN_V_DIM, 1))


def _ssd_kernel(xbc_ref, dt_ref, z_ref, cw_ref, cb_ref, alog_ref, dskip_ref, nw_ref, expand_ref,
                y_ref, tail_ref, state_ref, *, n_heads):
    c = pl.program_id(1)
    L = xbc_ref.shape[1]
    d_inner = n_heads * SSD_HEAD_DIM
    gw = d_inner // SSD_GROUPS
    hpg = n_heads // SSD_GROUPS

    @pl.when(c == 0)
    def _():
        tail_ref[...] = jnp.zeros_like(tail_ref)
        state_ref[...] = jnp.zeros_like(state_ref)

    u = xbc_ref[0]
    tail = tail_ref[...]
    row8 = lax.broadcasted_iota(I32, tail.shape, 0)
    conv = u * cw_ref[SSD_CONV - 1:SSD_CONV, :] + cb_ref[...]
    for s in range(1, SSD_CONV):
        r = pltpu.roll(u, s, axis=0)
        top = jnp.where(row8 < s, pltpu.roll(tail, s, axis=0), r[:SUBLANES])
        shifted = jnp.concatenate([top, r[SUBLANES:]], axis=0)
        conv = conv + shifted * cw_ref[SSD_CONV - 1 - s:SSD_CONV - s, :]
    tail_ref[...] = u[L - SUBLANES:, :]
    xbc = _silu(conv)
    xs = xbc[:, :d_inner]
    bmat = xbc[:, d_inner:d_inner + SSD_GROUPS * SSD_STATE]
    cmat = xbc[:, d_inner + SSD_GROUPS * SSD_STATE:]

    dt = dt_ref[0]
    a = dt * (-jnp.exp(alog_ref[...]))
    ri = lax.broadcasted_iota(I32, (L, L), 0)
    ci = lax.broadcasted_iota(I32, (L, L), 1)
    causal = ri >= ci
    tril = jnp.where(causal, 1.0, 0.0).astype(BF16)
    a_hi, a_mid, a_lo = _split3(a)
    a_cum = _dot(tril, a_hi) + _dot(tril, a_mid) + _dot(tril, a_lo)
    a_cum_t = a_cum.T
    a_last = a_cum[L - 1:L, :]
    ea = jnp.exp(a_cum)
    to_end = jnp.exp(a_last - a_cum)

    expand = expand_ref[...]

    def widen(v):
        hi, lo = _split2(v)
        return _dot(hi, expand) + _dot(lo, expand)

    dt_w = widen(dt)
    ea_w = widen(ea)
    te_w = widen(to_end)
    xdt = xs * dt_w
    xdt_b = xdt.astype(BF16)
    xte_b = (xdt * te_w).astype(BF16)

    y_parts = []
    for g in range(SSD_GROUPS):
        cg = cmat[:, g * SSD_STATE:(g + 1) * SSD_STATE]
        bg = bmat[:, g * SSD_STATE:(g + 1) * SSD_STATE]
        cg_b = cg.astype(BF16)
        cb = _dot_nt(cg_b, bg.astype(BF16))
        st = state_ref[g]
        y_off = _dot(cg_b, st.astype(BF16)) * ea_w[:, g * gw:(g + 1) * gw]
        y_diag = []
        for e in range(hpg):
            hh = g * hpg + e
            seg = a_cum[:, hh:hh + 1] - a_cum_t[hh:hh + 1, :]
            decay = jnp.exp(jnp.where(causal, seg, -jnp.inf))
            m = (cb * decay).astype(BF16)
            y_diag.append(_dot(m, xdt_b[:, hh * SSD_HEAD_DIM:(hh + 1) * SSD_HEAD_DIM]))
        y_parts.append(jnp.concatenate(y_diag, axis=1) + y_off)
        bt = bg.T.astype(BF16)
        state_ref[g] = st * ea_w[L - 1:L, g * gw:(g + 1) * gw] + _dot(bt, xte_b[:, g * gw:(g + 1) * gw])

    y = jnp.concatenate(y_parts, axis=1) + dskip_ref[...] * xs
    y = y * _silu(z_ref[0])
    outs = []
    for g in range(SSD_GROUPS):
        yg = y[:, g * gw:(g + 1) * gw]
        ms = jnp.mean(yg * yg, axis=-1, keepdims=True)
        outs.append(yg * lax.rsqrt(ms + NORM_EPS) * nw_ref[:, g * gw:(g + 1) * gw])
    y_ref[0] = jnp.concatenate(outs, axis=1).astype(y_ref.dtype)


def _ssd_call(xbc, dt, z, conv_w, conv_b, a_log, d_skip, ssd_norm):
    bsz, seq, d_conv = xbc.shape
    n_heads = a_log.shape[0]
    d_inner = n_heads * SSD_HEAD_DIM
    L = SSD_CHUNK
    alog = jnp.pad(a_log.astype(F32), (0, LANES - n_heads)).reshape(1, LANES)
    dskip = jnp.repeat(d_skip.astype(F32), SSD_HEAD_DIM).reshape(1, d_inner)
    head_of_lane = jnp.arange(d_inner) // SSD_HEAD_DIM
    expand = (jnp.arange(LANES)[:, None] == head_of_lane[None, :]).astype(BF16)
    const2 = lambda b, c: (0, 0)
    return pl.pallas_call(
        functools.partial(_ssd_kernel, n_heads=n_heads),
        grid=(bsz, seq // L),
        in_specs=[
            pl.BlockSpec((1, L, d_conv), lambda b, c: (b, c, 0)),
            pl.BlockSpec((1, L, LANES), lambda b, c: (b, c, 0)),
            pl.BlockSpec((1, L, d_inner), lambda b, c: (b, c, 0)),
            pl.BlockSpec((SSD_CONV, d_conv), const2),
            pl.BlockSpec((1, d_conv), const2),
            pl.BlockSpec((1, LANES), const2),
            pl.BlockSpec((1, d_inner), const2),
            pl.BlockSpec((1, d_inner), const2),
            pl.BlockSpec((LANES, d_inner), const2),
        ],
        out_specs=pl.BlockSpec((1, L, d_inner), lambda b, c: (b, c, 0)),
        out_shape=jax.ShapeDtypeStruct((bsz, seq, d_inner), BF16),
        scratch_shapes=[
            pltpu.VMEM((SUBLANES, d_conv), F32),
            pltpu.VMEM((SSD_GROUPS, SSD_STATE, d_inner // SSD_GROUPS), F32),
        ],
        compiler_params=_cparams(("arbitrary", "arbitrary")),
        name="ssd_scan",
    )(xbc, dt, z, conv_w, conv_b.reshape(1, d_conv), alog, dskip, ssd_norm.reshape(1, d_inner), expand)


def _merge_kernel(x_ref, ya_ref, ys_ref, gates_ref, mod_ref, nf_ref,
                  wba_ref, wbs_ref, wo_ref, wr_ref, wsg_ref, wsu_ref, wsd_ref,
                  h2_ref, base_ref, sc_ref):
    d = x_ref.shape[2]
    gate_m = mod_ref[0, 2:3, :]
    shift_f = mod_ref[0, 3:4, :]
    scale_f = mod_ref[0, 4:5, :]
    gate_f = mod_ref[0, 5:6, :]
    gates = gates_ref[0]
    mixed = gates[:, :d] * _dot(ya_ref[0], wba_ref[...]) + gates[:, d:] * _dot(ys_ref[0], wbs_ref[...])
    x1 = x_ref[0] + gate_m * _dot(mixed.astype(BF16), wo_ref[...])
    ms = jnp.mean(x1 * x1, axis=-1, keepdims=True)
    h2 = (x1 * lax.rsqrt(ms + NORM_EPS) * nf_ref[...]) * (1.0 + scale_f) + shift_f
    h2b = h2.astype(BF16)
    h2_ref[0] = _to_tiles(h2b)
    sc_ref[0] = _sigmoid(_dot(h2b, wr_ref[...]))
    act = _silu(_dot(h2b, wsg_ref[...])) * _dot(h2b, wsu_ref[...])
    shared = _dot(act.astype(BF16), wsd_ref[...])
    base_ref[0] = x1 + gate_f * shared


def _merge_call(x, y_attn, y_ssd, gates, mod3, norm_ffn, w_ba, w_bs, w_out, w_router, w_sg, w_su, w_sd, tm):
    bsz, seq, d = x.shape
    ws = [w.astype(BF16) for w in (w_ba, w_bs, w_out, w_router, w_sg, w_su, w_sd)]
    const2 = lambda b, i: (0, 0)
    wspec = lambda w: pl.BlockSpec(w.shape, const2, pipeline_mode=pl.Buffered(1))
    row_spec = lambda n: pl.BlockSpec((1, tm, n), lambda b, i: (b, i, 0))
    return pl.pallas_call(
        _merge_kernel,
        grid=(bsz, seq // tm),
        in_specs=[row_spec(d), row_spec(y_attn.shape[2]), row_spec(y_ssd.shape[2]), row_spec(2 * d),
                  pl.BlockSpec((1, 6, d), lambda b, i: (b, 0, 0)), pl.BlockSpec((1, d), const2)]
                 + [wspec(w) for w in ws],
        out_specs=[pl.BlockSpec((1, tm, d // LANES, LANES), lambda b, i: (b, i, 0, 0)), row_spec(d),
                   row_spec(N_EXPERTS)],
        out_shape=[jax.ShapeDtypeStruct((bsz, seq, d // LANES, LANES), BF16),
                   jax.ShapeDtypeStruct((bsz, seq, d), F32), jax.ShapeDtypeStruct((bsz, seq, N_EXPERTS), F32)],
        compiler_params=_cparams(("arbitrary", "arbitrary")),
        name="merge_ffn_pre",
    )(x, y_attn, y_ssd, gates, mod3, norm_ffn.reshape(1, d), *ws)


def _first_max(v, iota, sentinel):
    mx = jnp.max(v, axis=0, keepdims=True)
    first = jnp.min(jnp.where(v == mx, iota, sentinel), axis=0, keepdims=True)
    return mx, first


def _route_kernel(sc_ref, bias_ref, idx_ref, w_ref):
    sc = sc_ref[...].T
    tm = sc.shape[1]
    choice = sc + bias_ref[...]
    per_group = N_EXPERTS // N_EXPERT_GROUPS
    gi = lax.broadcasted_iota(I32, (per_group, tm), 0)
    gscore = []
    for g in range(N_EXPERT_GROUPS):
        cg = choice[g * per_group:(g + 1) * per_group, :]
        m1, f1 = _first_max(cg, gi, per_group)
        m2 = jnp.max(jnp.where(gi == f1, -jnp.inf, cg), axis=0, keepdims=True)
        gscore.append(m1 + m2)
    cur = jnp.concatenate(gscore, axis=0)
    giota = lax.broadcasted_iota(I32, cur.shape, 0)
    keep = jnp.zeros(cur.shape, F32)
    for _ in range(TOPK_GROUPS):
        _, f = _first_max(cur, giota, N_EXPERT_GROUPS)
        sel = giota == f
        keep = jnp.where(sel, 1.0, keep)
        cur = jnp.where(sel, -jnp.inf, cur)
    masked = jnp.concatenate(
        [jnp.where(keep[g:g + 1, :] > 0.5, choice[g * per_group:(g + 1) * per_group, :], -jnp.inf)
         for g in range(N_EXPERT_GROUPS)], axis=0)
    eiota = lax.broadcasted_iota(I32, masked.shape, 0)
    ids, ws = [], []
    for _ in range(TOP_K):
        _, f = _first_max(masked, eiota, N_EXPERTS)
        sel = eiota == f
        ids.append(f)
        ws.append(jnp.sum(jnp.where(sel, sc, 0.0), axis=0, keepdims=True))
        masked = jnp.where(sel, -jnp.inf, masked)
    idx_ref[...] = jnp.concatenate(ids, axis=0)
    w = jnp.concatenate(ws, axis=0)
    w = w / (jnp.sum(w, axis=0, keepdims=True) + 1e-20) * ROUTED_SCALE
    cols = [jnp.broadcast_to(w[k:k + 1, :], (LANES, tm)).T for k in range(TOP_K)]
    w_ref[...] = _to_tiles(jnp.concatenate(cols, axis=1))


def _route_call(scores, router_bias, tm):
    n = scores.shape[0]
    return pl.pallas_call(
        _route_kernel,
        grid=(n // tm,),
        in_specs=[pl.BlockSpec((tm, N_EXPERTS), lambda i: (i, 0)), pl.BlockSpec((N_EXPERTS, 1), lambda i: (0, 0))],
        out_specs=[pl.BlockSpec((TOP_K, tm), lambda i: (0, i)), pl.BlockSpec((tm, TOP_K, LANES), lambda i: (i, 0, 0))],
        out_shape=[jax.ShapeDtypeStruct((TOP_K, n), I32), jax.ShapeDtypeStruct((n, TOP_K, LANES), F32)],
        compiler_params=_cparams(("arbitrary",)),
        name="route_topk",
    )(scores, router_bias.astype(F32).reshape(N_EXPERTS, 1))


def _rank_kernel(idx_ref, dest_ref, cnt_ref, base_ref, *, blk):
    phase = pl.program_id(0)
    i = pl.program_id(1)
    idx = idx_ref[...]
    tm = idx.shape[1]
    eiota = lax.broadcasted_iota(I32, (N_EXPERTS, tm), 0)
    onehot = jnp.zeros((N_EXPERTS, tm), F32)
    for k in range(TOP_K):
        onehot = onehot + jnp.where(eiota == idx[k:k + 1, :], 1.0, 0.0)

    @pl.when((phase == 0) & (i == 0))
    def _():
        base_ref[...] = jnp.zeros_like(base_ref)

    @pl.when((phase == 1) & (i == 0))
    def _():
        counts = base_ref[...]
        cnt_ref[...] = counts
        padded = jnp.floor((counts + (blk - 1)) * (1.0 / blk)) * blk
        ri = lax.broadcasted_iota(I32, (N_EXPERTS, N_EXPERTS), 0)
        ci = lax.broadcasted_iota(I32, (N_EXPERTS, N_EXPERTS), 1)
        lower = jnp.where(ci < ri, 1.0, 0.0).astype(BF16)
        hi, mid, lo = _split3(padded)
        base_ref[...] = _dot(lower, hi) + _dot(lower, mid) + _dot(lower, lo)

    @pl.when(phase == 1)
    def _():
        ri = lax.broadcasted_iota(I32, (tm, tm), 0)
        ci = lax.broadcasted_iota(I32, (tm, tm), 1)
        before = jnp.where(ri < ci, 1.0, 0.0).astype(BF16)
        prior = _dot(onehot.astype(BF16), before) + base_ref[:, 0:1]
        rows = []
        for k in range(TOP_K):
            rows.append(jnp.sum(jnp.where(eiota == idx[k:k + 1, :], prior, 0.0), axis=0, keepdims=True))
        dest_ref[...] = jnp.concatenate(rows, axis=0).astype(I32)

    base_ref[...] = base_ref[...] + jnp.sum(onehot, axis=1, keepdims=True)


def _rank_call(idx, tm, blk):
    n = idx.shape[1]
    return pl.pallas_call(
        functools.partial(_rank_kernel, blk=blk),
        grid=(2, n // tm),
        in_specs=[pl.BlockSpec((TOP_K, tm), lambda p, i: (0, i))],
        out_specs=[pl.BlockSpec((TOP_K, tm), lambda p, i: (0, i * p)),
                   pl.BlockSpec((N_EXPERTS, LANES), lambda p, i: (0, 0))],
        out_shape=[jax.ShapeDtypeStruct((TOP_K, n), I32), jax.ShapeDtypeStruct((N_EXPERTS, LANES), F32)],
        scratch_shapes=[pltpu.VMEM((N_EXPERTS, LANES), F32)],
        compiler_params=_cparams(("arbitrary", "arbitrary")),
        name="expert_rank",
    )(idx)


def _dispatch_kernel(pend_ref, padded_ref, nu_ref, dest_ref, h_ref, xs_ref, zero_ref, sem_z, sem_s, *, blk):
    i = pl.program_id(0)
    tm = h_ref.shape[0]
    n_blocks = xs_ref.shape[0] // blk

    def zero_copy(start):
        return pltpu.make_async_copy(zero_ref, xs_ref.at[pl.ds(pl.multiple_of(start, blk), blk)], sem_z)

    @pl.when(i == 0)
    def _():
        zero_ref[...] = jnp.zeros_like(zero_ref)

        def start(e, _):
            @pl.when(padded_ref[e] > 0)
            def _():
                zero_copy(pend_ref[e] - blk).start()
            return 0

        def wait(e, _):
            @pl.when(padded_ref[e] > 0)
            def _():
                zero_copy(pend_ref[e] - blk).wait()
            return 0

        def start_idle(b, _):
            zero_copy(b * blk).start()
            return 0

        def wait_idle(b, _):
            zero_copy(b * blk).wait()
            return 0

        lax.fori_loop(0, N_EXPERTS, start, 0)
        lax.fori_loop(nu_ref[0], n_blocks, start_idle, 0)
        lax.fori_loop(0, N_EXPERTS, wait, 0)
        lax.fori_loop(nu_ref[0], n_blocks, wait_idle, 0)

    def issue(g, _):
        for u in range(ISSUE_UNROLL):
            t = g * ISSUE_UNROLL + u
            for k in range(TOP_K):
                pltpu.make_async_copy(h_ref.at[t], xs_ref.at[dest_ref[k, t]], sem_s).start(priority=k % 2)
        return 0

    lax.fori_loop(0, tm // ISSUE_UNROLL, issue, 0)
    for k in range(TOP_K):
        pltpu.make_async_copy(h_ref, xs_ref.at[pl.ds(0, tm)], sem_s).wait()


def _dispatch_call(pend, padded, n_used, dest, h2t, n_slots, blk, tm):
    n, s, lanes = h2t.shape
    return pl.pallas_call(
        functools.partial(_dispatch_kernel, blk=blk),
        grid_spec=pltpu.PrefetchScalarGridSpec(
            num_scalar_prefetch=3,
            grid=(n // tm,),
            in_specs=[
                pl.BlockSpec((TOP_K, tm), lambda i, pe, pa, nu: (0, i), memory_space=pltpu.SMEM),
                pl.BlockSpec((tm, s, lanes), lambda i, pe, pa, nu: (i, 0, 0)),
            ],
            out_specs=pl.BlockSpec(memory_space=pl.ANY),
            scratch_shapes=[pltpu.VMEM((blk, s, lanes), h2t.dtype), pltpu.SemaphoreType.DMA(()),
                            pltpu.SemaphoreType.DMA(())],
        ),
        out_shape=jax.ShapeDtypeStruct((n_slots, s, lanes), h2t.dtype),
        compiler_params=_cparams(("arbitrary",)),
        name="moe_dispatch",
    )(pend, padded, n_used, dest, h2t)


RING = 3


def _expert_kernel(be_ref, nu_ref, first_ref, wslot_ref, nxt1_ref, nxt2_ref, xs_hbm, wg_hbm, wu_hbm, wd_hbm, y_hbm,
                   xbuf, ybuf, zbuf, wg_f, wu_f, wd_f, wg_b, wu_b, wd_b, x_sems, y_sems, w_sems, *, blk):
    i = pl.program_id(0)
    nu = nu_ref[0]
    last = pl.num_programs(0) - 1

    def x_copy(b):
        slot = b % RING
        return pltpu.make_async_copy(xs_hbm.at[pl.ds(pl.multiple_of(b * blk, blk), blk)], xbuf.at[slot],
                                     x_sems.at[slot])

    def y_copy(b, src):
        return pltpu.make_async_copy(src, y_hbm.at[pl.ds(pl.multiple_of(b * blk, blk), blk)], y_sems.at[b % RING])

    def weight_copies(e, slot):
        return (pltpu.make_async_copy(wg_hbm.at[e], wg_f.at[slot], w_sems.at[slot]),
                pltpu.make_async_copy(wu_hbm.at[e], wu_f.at[slot], w_sems.at[slot]),
                pltpu.make_async_copy(wd_hbm.at[e], wd_f.at[slot], w_sems.at[slot]))

    @pl.when(i == 0)
    def _():
        zbuf[...] = jnp.zeros_like(zbuf)
        x_copy(0).start()

        @pl.when(nu > 1)
        def _():
            x_copy(1).start()

        for cp in weight_copies(be_ref[0], 0):
            cp.start()

        @pl.when(nxt1_ref[0] >= 0)
        def _():
            for cp in weight_copies(nxt1_ref[0], 1):
                cp.start()

    @pl.when(i < nu)
    def _():
        @pl.when(i + 2 < nu)
        def _():
            x_copy(i + 2).start()

        @pl.when(first_ref[i] == 1)
        def _():
            slot = wslot_ref[i]

            @pl.when(nxt2_ref[i] >= 0)
            def _():
                for cp in weight_copies(nxt2_ref[i], (slot + 2) % RING):
                    cp.start()

            for cp in weight_copies(be_ref[i], slot):
                cp.wait()
            wg_b[...] = wg_f[slot].astype(BF16)
            wu_b[...] = wu_f[slot].astype(BF16)
            wd_b[...] = wd_f[slot].astype(BF16)

        x_copy(i).wait()
        slot = i % RING
        half = blk // 2
        xs_h = [_from_tiles(xbuf[slot, r * half:(r + 1) * half]) for r in range(2)]
        gu = [(_dot(xh, wg_b[...]), _dot(xh, wu_b[...])) for xh in xs_h]
        for r in range(2):
            act = _silu(gu[r][0]) * gu[r][1]
            ybuf[slot, r * half:(r + 1) * half] = _to_tiles(_dot(act.astype(BF16), wd_b[...]).astype(ybuf.dtype))
        y_copy(i, ybuf.at[slot]).start()

    @pl.when(i >= nu)
    def _():
        y_copy(i, zbuf).start()

    @pl.when(i >= 2)
    def _():
        y_copy(i - 2, zbuf).wait()

    @pl.when(i == last)
    def _():
        @pl.when(i >= 1)
        def _():
            y_copy(i - 1, zbuf).wait()

        y_copy(i, zbuf).wait()


def _expert_call(block_e, n_used, first, wslot, nxt1, nxt2, xs, w_gate, w_up, w_down, blk):
    n_slots, s, lanes = xs.shape
    _, d, ff = w_gate.shape
    n_blocks = n_slots // blk
    hbm = pl.BlockSpec(memory_space=pl.ANY)
    return pl.pallas_call(
        functools.partial(_expert_kernel, blk=blk),
        grid_spec=pltpu.PrefetchScalarGridSpec(
            num_scalar_prefetch=6,
            grid=(n_blocks,),
            in_specs=[hbm, hbm, hbm, hbm],
            out_specs=hbm,
            scratch_shapes=[pltpu.VMEM((RING, blk, s, lanes), xs.dtype), pltpu.VMEM((RING, blk, s, lanes), BF16),
                            pltpu.VMEM((blk, s, lanes), BF16),
                            pltpu.VMEM((RING, d, ff), F32), pltpu.VMEM((RING, d, ff), F32),
                            pltpu.VMEM((RING, ff, d), F32),
                            pltpu.VMEM((d, ff), BF16), pltpu.VMEM((d, ff), BF16), pltpu.VMEM((ff, d), BF16),
                            pltpu.SemaphoreType.DMA((RING,)), pltpu.SemaphoreType.DMA((RING,)),
                            pltpu.SemaphoreType.DMA((RING,))],
        ),
        out_shape=jax.ShapeDtypeStruct((n_slots, s, lanes), BF16),
        compiler_params=_cparams(("arbitrary",)),
        name="moe_experts",
    )(block_e, n_used, first, wslot, nxt1, nxt2, xs, w_gate, w_up, w_down)


def _combine_kernel(dest_ref, dnext_ref, w_ref, base_ref, gf_ref, nfin_ref, y_ref, o_ref, buf_ref, sems, *,
                    final_norm):
    i = pl.program_id(0)
    last = pl.num_programs(0) - 1
    tc = base_ref.shape[0]

    def issue_all(d_ref, slot):
        def issue(g, _):
            for u in range(ISSUE_UNROLL):
                t = g * ISSUE_UNROLL + u
                for k in range(TOP_K):
                    pltpu.make_async_copy(y_ref.at[d_ref[k, t]], buf_ref.at[slot, k, t],
                                          sems.at[slot]).start(priority=k % 2)
            return 0

        lax.fori_loop(0, tc // ISSUE_UNROLL, issue, 0)

    @pl.when(i == 0)
    def _():
        issue_all(dest_ref, 0)

    for par in range(2):
        @pl.when((i < last) & (i % 2 == par))
        def _():
            issue_all(dnext_ref, 1 - par)

    for par in range(2):
        @pl.when(i % 2 == par)
        def _():
            for k in range(TOP_K):
                pltpu.make_async_copy(y_ref.at[pl.ds(0, tc)], buf_ref.at[par, k], sems.at[par]).wait()

    slot = i % 2
    routed = buf_ref[slot, 0].astype(F32) * w_ref[:, 0:1, :]
    for k in range(1, TOP_K):
        routed = routed + buf_ref[slot, k].astype(F32) * w_ref[:, k:k + 1, :]
    x2 = base_ref[...] + gf_ref[0] * _from_tiles(routed)
    if final_norm:
        ms = jnp.mean(x2 * x2, axis=-1, keepdims=True)
        x2 = x2 * lax.rsqrt(ms + NORM_EPS) * nfin_ref[...]
    o_ref[...] = x2


def _combine_call(dest, w_tok, base, gate_f, norm_final, y, tokens_per_batch, tc, final_norm):
    n, d = base.shape
    steps_per_batch = tokens_per_batch // tc
    steps = n // tc
    return pl.pallas_call(
        functools.partial(_combine_kernel, final_norm=final_norm),
        grid=(steps,),
        in_specs=[
            pl.BlockSpec((TOP_K, tc), lambda i: (0, i), memory_space=pltpu.SMEM),
            pl.BlockSpec((TOP_K, tc), lambda i: (0, jnp.minimum(i + 1, steps - 1)), memory_space=pltpu.SMEM),
            pl.BlockSpec((tc, TOP_K, LANES), lambda i: (i, 0, 0)),
            pl.BlockSpec((tc, d), lambda i: (i, 0)),
            pl.BlockSpec((1, 1, d), lambda i: (i // steps_per_batch, 0, 0)),
            pl.BlockSpec((1, d), lambda i: (0, 0)),
            pl.BlockSpec(memory_space=pl.ANY),
        ],
        out_specs=pl.BlockSpec((tc, d), lambda i: (i, 0)),
        out_shape=jax.ShapeDtypeStruct((n, d), F32),
        scratch_shapes=[pltpu.VMEM((2, TOP_K, tc, d // LANES, LANES), y.dtype), pltpu.SemaphoreType.DMA((2,))],
        compiler_params=_cparams(("arbitrary",)),
        name="moe_combine",
    )(dest, dest, w_tok, base, gate_f, norm_final.reshape(1, d), y)


def _moe_blocks(n_pairs, blk):
    return -(-n_pairs // blk) + N_EXPERTS


def _moe_layout(counts, n_pairs, blk):
    n_blocks = _moe_blocks(n_pairs, blk)
    padded = (counts + blk - 1) // blk * blk
    pend = jnp.cumsum(padded)
    n_used = pend[-1] // blk
    blk_start = jnp.arange(n_blocks, dtype=I32) * blk
    block_e = jnp.sum((pend[None, :] <= blk_start[:, None]).astype(I32), axis=1)
    block_e = jnp.minimum(block_e, N_EXPERTS - 1)
    used = jnp.arange(n_blocks) < n_used
    prev_e = jnp.concatenate([jnp.full((1,), -1, I32), block_e[:-1]])
    first = (used & (block_e != prev_e)).astype(I32)
    has = padded > 0
    ids = jnp.arange(N_EXPERTS, dtype=I32)
    ordinal = jnp.cumsum(has.astype(I32)) - 1
    at_or_after = lax.cummin(jnp.where(has, ids, N_EXPERTS), axis=0, reverse=True)
    after = jnp.concatenate([at_or_after[1:], jnp.full((2,), N_EXPERTS, I32)])
    nxt1_e = after[:N_EXPERTS]
    nxt2_e = after[jnp.minimum(nxt1_e, N_EXPERTS)]
    fix = lambda v: jnp.where(v >= N_EXPERTS, -1, v).astype(I32)
    wslot = (ordinal % RING).astype(I32)[block_e]
    nxt1 = fix(nxt1_e)[block_e]
    nxt2 = fix(nxt2_e)[block_e]
    return (pend.astype(I32), padded.astype(I32), block_e.astype(I32), n_used.reshape(1).astype(I32),
            first, wslot, nxt1, nxt2, n_blocks)


def kernel(x, c, positions, w_ada, b_ada, norm_mix, w_in, b_gate, lambda_q1, lambda_k1, lambda_q2, lambda_k2,
           attn_head_norm, conv_w, conv_b, dt_bias, a_log, d_skip, ssd_norm, w_branch_attn, w_branch_ssd, w_out,
           norm_ffn, w_router, router_bias, w_exp_gate, w_exp_up, w_exp_down, w_sh_gate, w_sh_up, w_sh_down,
           norm_final):
    bsz, seq, d = x.shape
    n = bsz * seq
    depth = w_ada.shape[0]
    tm = min(TOKEN_TILE, seq)
    c_pad = jnp.pad(c, ((0, -bsz % SUBLANES), (0, 0)))
    for l in range(depth):
        last = l == depth - 1
        mod3 = _ada_call(c_pad, w_ada[l], b_ada[l])[:bsz].reshape(bsz, 6, d)
        qt4, k, vt4, z, xbc, gates, dt = _inproj_call(x, positions, mod3, norm_mix[l], b_gate[l], dt_bias[l],
                                                      w_in[l], tm)
        lambda_init = 0.8 - 0.6 * math.exp(-0.3 * l)
        lam_vecs = jnp.stack([lambda_q1[l], lambda_k1[l], lambda_q2[l], lambda_k2[l]]).astype(F32)
        y_attn = _attn_call(qt4, k, vt4, lam_vecs, attn_head_norm[l].astype(F32), lambda_init, ATTN_HEADS_PER_STEP)
        y_ssd = _ssd_call(xbc, dt, z, conv_w[l], conv_b[l], a_log[l], d_skip[l], ssd_norm[l])
        h2, base, scores = _merge_call(x, y_attn, y_ssd, gates, mod3, norm_ffn[l], w_branch_attn[l],
                                       w_branch_ssd[l], w_out[l], w_router[l], w_sh_gate[l], w_sh_up[l],
                                       w_sh_down[l], tm)
        rt = min(ROUTE_TILE, n)
        idx, w_tiles = _route_call(scores.reshape(n, N_EXPERTS), router_bias[l], rt)
        dest, cnt = _rank_call(idx, rt, MOE_BLOCK)
        pend, padded, block_e, n_used, first, wslot, nxt1, nxt2, n_blocks = _moe_layout(
            cnt[:, 0].astype(I32), n * TOP_K, MOE_BLOCK)
        xs = _dispatch_call(pend, padded, n_used, dest, h2.reshape(n, d // LANES, LANES), n_blocks * MOE_BLOCK,
                            MOE_BLOCK, min(DISPATCH_TILE, seq))
        y = _expert_call(block_e, n_used, first, wslot, nxt1, nxt2, xs, w_exp_gate[l], w_exp_up[l], w_exp_down[l],
                         MOE_BLOCK)
        out = _combine_call(dest, w_tiles, base.reshape(n, d), mod3[:, 5:6, :], norm_final, y, seq,
                            min(COMBINE_TILE, seq), final_norm=last)
        x = out.reshape(bsz, seq, d)
    return x
```

```python
import functools
import math

import jax
import jax.numpy as jnp
from jax import lax
from jax.experimental import pallas as pl
from jax.experimental.pallas import tpu as pltpu

F32 = jnp.float32
BF16 = jnp.bfloat16
I32 = jnp.int32

ATTN_HEADS = 8
ATTN_QK_DIM = 64
ATTN_V_DIM = 128
ROPE_DIM = 16
ROPE_THETA = 500000.0
SSD_HEAD_DIM = 64
SSD_GROUPS = 4
SSD_STATE = 128
SSD_CONV = 4
SSD_CHUNK = 128
N_EXPERTS = 256
TOP_K = 8
N_EXPERT_GROUPS = 8
TOPK_GROUPS = 4
ROUTED_SCALE = 2.5
NORM_EPS = 1e-6
LOG2E = 1.4426950408889634
NEG_BIG = -1e30

LANES = 128
SUBLANES = 8
VMEM_LIMIT_BYTES = 56 * 1024 * 1024

TOKEN_TILE = 256
MERGE_TILE = 512
MOE_BLOCK = 256
ROUTE_TILE = 1024
DISPATCH_TILE = 512
COMBINE_TILE = 256
ATTN_HEADS_PER_STEP = 4
ISSUE_UNROLL = 4


def _cparams(sem, vmem=VMEM_LIMIT_BYTES):
    return pltpu.CompilerParams(dimension_semantics=sem, vmem_limit_bytes=vmem)


def _dot(a, b):
    return jnp.dot(a, b, preferred_element_type=F32)


def _dot_nt(a, b):
    return lax.dot_general(a, b, (((1,), (1,)), ((), ())), preferred_element_type=F32)


def _sigmoid(x):
    return 1.0 / (1.0 + jnp.exp(-x))


def _silu(x):
    return x * _sigmoid(x)


def _to_tiles(x):
    return x.reshape(x.shape[0], x.shape[1] // LANES, LANES)


def _from_tiles(x):
    return x.reshape(x.shape[0], x.shape[1] * x.shape[2])


def _split2(x):
    hi = x.astype(BF16)
    lo = (x - hi.astype(F32)).astype(BF16)
    return hi, lo


def _split3(x):
    hi = x.astype(BF16)
    r = x - hi.astype(F32)
    mid = r.astype(BF16)
    lo = (r - mid.astype(F32)).astype(BF16)
    return hi, mid, lo


def _ada_kernel(c_ref, w_ref, b_ref, o_ref):
    o_ref[...] = _dot(_silu(c_ref[...]).astype(BF16), w_ref[...].astype(BF16)) + b_ref[...]


def _ada_call(c_pad, w_ada, b_ada):
    rows, d = c_pad.shape
    n = w_ada.shape[1]
    tn = 1024
    return pl.pallas_call(
        _ada_kernel,
        grid=(n // tn,),
        in_specs=[
            pl.BlockSpec((rows, d), lambda j: (0, 0)),
            pl.BlockSpec((d, tn), lambda j: (0, j)),
            pl.BlockSpec((1, tn), lambda j: (0, j)),
        ],
        out_specs=pl.BlockSpec((rows, tn), lambda j: (0, j)),
        out_shape=jax.ShapeDtypeStruct((rows, n), F32),
        compiler_params=_cparams(("arbitrary",)),
        name="ada_mod",
    )(c_pad, w_ada, b_ada.reshape(1, n))


def _inproj_kernel(x_ref, pos_ref, mod_ref, nm_ref, freq_ref, sign_ref, dtb_ref, bg_ref,
                   wq_ref, wk_ref, wv_ref, wz_ref, wx_ref, wg_ref, wdt_ref,
                   qt_ref, k_ref, vt_ref, z_ref, xbc_ref, gates_ref, dt_ref, *, q_scale):
    x = x_ref[0]
    ms = jnp.mean(x * x, axis=-1, keepdims=True)
    shift = mod_ref[0, 0:1, :]
    scale = mod_ref[0, 1:2, :]
    h = (x * lax.rsqrt(ms + NORM_EPS) * nm_ref[...]) * (1.0 + scale) + shift
    hb = h.astype(BF16)

    ang = pos_ref[0].astype(F32) * freq_ref[...]
    cos = jnp.cos(ang)
    sin_signed = jnp.sin(ang) * sign_ref[...]
    lane = lax.broadcasted_iota(I32, ang.shape, 1) % ATTN_QK_DIM
    first_half = lane < (ROPE_DIM // 2)

    def rope(t):
        outs = []
        for hd in range(t.shape[1] // LANES):
            th = t[:, hd * LANES:(hd + 1) * LANES]
            up = pltpu.roll(th, LANES - ROPE_DIM // 2, axis=1)
            down = pltpu.roll(th, ROPE_DIM // 2, axis=1)
            partner = jnp.where(first_half, up, down)
            outs.append(th * cos + partner * sin_signed)
        return jnp.concatenate(outs, axis=1)

    q = rope(_dot(hb, wq_ref[...])) * q_scale
    qt_ref[0, 0] = q.T.astype(BF16)
    k = rope(_dot(hb, wk_ref[...]))
    k_ref[0] = k.astype(BF16)
    v = _dot(hb, wv_ref[...])
    vt_ref[0, 0] = v.T.astype(BF16)
    z_ref[0] = _dot(hb, wz_ref[...])
    xbc_ref[0] = _dot(hb, wx_ref[...])
    g = _dot(hb, wg_ref[...]) + bg_ref[...]
    gates_ref[0] = _sigmoid(g)
    dtr = _dot(hb, wdt_ref[...]) + dtb_ref[...]
    dt_ref[0] = jnp.maximum(dtr, 0.0) + jnp.log1p(jnp.exp(-jnp.abs(dtr)))


def _inproj_call(x, positions, mod3, norm_mix, b_gate, dt_bias, w_in, tm):
    bsz, seq, d = x.shape
    dq = ATTN_HEADS * 2 * ATTN_QK_DIM
    dv = ATTN_HEADS * ATTN_V_DIM
    n_heads_ssd = dt_bias.shape[0]
    d_inner = n_heads_ssd * SSD_HEAD_DIM
    d_conv = d_inner + 2 * SSD_GROUPS * SSD_STATE
    sizes = (dq, dq, dv, d_inner, d_conv, n_heads_ssd, 2 * d)
    offs = [0]
    for s in sizes:
        offs.append(offs[-1] + s)
    wb = w_in.astype(BF16)
    wq, wk, wv, wz, wx, wdt, wg = (wb[:, offs[i]:offs[i + 1]] for i in range(7))
    wdt = jnp.pad(wdt, ((0, 0), (0, LANES - n_heads_ssd)))
    dtb = jnp.pad(dt_bias.astype(F32), (0, LANES - n_heads_ssd)).reshape(1, LANES)

    half = ROPE_DIM // 2
    lane_d = jnp.arange(LANES) % ATTN_QK_DIM
    inv_freq = 1.0 / (ROPE_THETA ** (jnp.arange(0, ROPE_DIM, 2, dtype=F32) / ROPE_DIM))
    freq = jnp.where(lane_d < ROPE_DIM, inv_freq[lane_d % half], 0.0).reshape(1, LANES).astype(F32)
    sign = jnp.where(lane_d < half, -1.0, jnp.where(lane_d < ROPE_DIM, 1.0, 0.0)).reshape(1, LANES).astype(F32)

    const2 = lambda b, i: (0, 0)
    wspec = lambda w: pl.BlockSpec(w.shape, const2, pipeline_mode=pl.Buffered(1))
    row_spec = lambda n: pl.BlockSpec((1, tm, n), lambda b, i: (b, i, 0))
    col_spec = lambda n: pl.BlockSpec((1, 1, n, tm), lambda b, i: (b, i, 0, 0))
    q_scale = (ATTN_QK_DIM ** -0.5) * LOG2E
    return pl.pallas_call(
        functools.partial(_inproj_kernel, q_scale=q_scale),
        grid=(bsz, seq // tm),
        in_specs=[
            row_spec(d),
            pl.BlockSpec((1, tm, 1), lambda b, i: (b, i, 0)),
            pl.BlockSpec((1, 6, d), lambda b, i: (b, 0, 0)),
            pl.BlockSpec((1, d), const2),
            pl.BlockSpec((1, LANES), const2),
            pl.BlockSpec((1, LANES), const2),
            pl.BlockSpec((1, LANES), const2),
            pl.BlockSpec((1, 2 * d), const2),
            wspec(wq), wspec(wk), wspec(wv), wspec(wz), wspec(wx), wspec(wg), wspec(wdt),
        ],
        out_specs=[col_spec(dq), row_spec(dq), col_spec(dv), row_spec(d_inner), row_spec(d_conv),
                   row_spec(2 * d), row_spec(LANES)],
        out_shape=[
            jax.ShapeDtypeStruct((bsz, seq // tm, dq, tm), BF16),
            jax.ShapeDtypeStruct((bsz, seq, dq), BF16),
            jax.ShapeDtypeStruct((bsz, seq // tm, dv, tm), BF16),
            jax.ShapeDtypeStruct((bsz, seq, d_inner), F32),
            jax.ShapeDtypeStruct((bsz, seq, d_conv), F32),
            jax.ShapeDtypeStruct((bsz, seq, 2 * d), F32),
            jax.ShapeDtypeStruct((bsz, seq, LANES), F32),
        ],
        compiler_params=_cparams(("arbitrary", "arbitrary")),
        name="in_proj",
    )(x, positions.reshape(bsz, seq, 1), mod3, norm_mix.reshape(1, d), freq, sign, dtb,
      b_gate.reshape(1, 2 * d), wq, wk, wv, wz, wx, wg, wdt)


def _attn_kernel(qt_ref, qtn_ref, k_ref, vt_ref, lam_ref, gain_ref, o_ref, acc_ref, m_ref, l_ref, qm_ref, sa_ref,
                 sb_ref, off_ref, *, tq, heads, lambda_init):
    i = pl.program_id(2)
    tk = 2 * tq
    row = lax.broadcasted_iota(I32, (LANES, tq), 0)

    def build_qm(src_ref, slot):
        for h in range(heads):
            qt = src_ref[0, 0, h * LANES:(h + 1) * LANES, :]
            zero = jnp.zeros_like(qt)
            qm_ref[slot, h] = jnp.concatenate([jnp.where(row < ATTN_QK_DIM, qt, zero),
                                               jnp.where(row >= ATTN_QK_DIM, qt, zero)], axis=1)

    def scores_into(pair, dst_ref, hs, qslot):
        koff = pl.multiple_of(pair * tk, tk)
        for h in hs:
            dst_ref[h] = _dot(k_ref[0, pl.ds(koff, tk), h * LANES:(h + 1) * LANES], qm_ref[qslot, h])

    all_heads = tuple(range(heads))

    @pl.when(i == 0)
    def _():
        build_qm(qt_ref, 0)
        scores_into(0, sa_ref, all_heads, 0)
        off_ref[0] = 0

    acc_ref[...] = jnp.zeros_like(acc_ref)
    m_ref[...] = jnp.full(m_ref.shape, NEG_BIG, F32)
    l_ref[...] = jnp.zeros_like(l_ref)
    qslot = i % 2
    off = off_ref[0]

    def softmax_pv(pair, src_ref, masked, hs, nblk=2):
        for h in hs:
            s = src_ref[h, :nblk * tq, :]
            if masked:
                kidx = pair * tk + lax.broadcasted_iota(I32, s.shape, 0)
                qidx = i * tq + lax.broadcasted_iota(I32, s.shape, 1) % tq
                s = jnp.where(kidx <= qidx, s, NEG_BIG)
            m_old = m_ref[h]
            m_new = jnp.maximum(m_old, jnp.max(s, axis=0, keepdims=True))
            alpha = jnp.exp2(m_old - m_new)
            p = jnp.exp2(s - m_new)
            l_ref[h] = alpha * l_ref[h] + jnp.sum(p, axis=0, keepdims=True)
            m_ref[h] = m_new
            pb = p.astype(BF16)
            pv = _dot(vt_ref[0, 2 * pair, h * LANES:(h + 1) * LANES, :], pb[:tq])
            if nblk == 2:
                pv = pv + _dot(vt_ref[0, 2 * pair + 1, h * LANES:(h + 1) * LANES, :], pb[tq:])
            acc_ref[h] = acc_ref[h] * alpha + pv

    bufs = (sa_ref, sb_ref)
    n_full = i // 2

    def body(n, carry):
        for par in range(2):
            @pl.when((n + off) % 2 == par)
            def _():
                for h in all_heads:
                    scores_into(n + 1, bufs[1 - par], (h,), qslot)
                    softmax_pv(n, bufs[par], False, (h,))
        return carry

    lax.fori_loop(0, n_full, body, 0)
    for par in range(2):
        for odd in range(2):
            @pl.when(((n_full + off) % 2 == par) & (i % 2 == odd))
            def _():
                build_qm(qtn_ref, 1 - odd)
                for h in all_heads:
                    scores_into(0, bufs[1 - par], (h,), 1 - odd)
                    softmax_pv(n_full, bufs[par], True, (h,), nblk=1 + odd)
                off_ref[0] = 1 - par

    lv = lam_ref[...]
    lam = (jnp.exp(jnp.sum(lv[0:1] * lv[1:2], axis=1, keepdims=True))
           - jnp.exp(jnp.sum(lv[2:3] * lv[3:4], axis=1, keepdims=True)) + lambda_init)
    for h in range(heads):
        acc = acc_ref[h]
        l = l_ref[h]
        o = acc[:, :tq] / l[:, :tq] - lam * (acc[:, tq:] / l[:, tq:])
        ms = jnp.mean(o * o, axis=0, keepdims=True)
        o = o * lax.rsqrt(ms + NORM_EPS) * gain_ref[...] * (1.0 - lambda_init)
        o_ref[0, :, h * LANES:(h + 1) * LANES] = o.T.astype(o_ref.dtype)


def _attn_call(qt4, k, vt4, lam_vecs, head_gain, lambda_init, heads):
    bsz, nq, dq, tq = qt4.shape
    seq = k.shape[1]
    dv = vt4.shape[2]
    w = heads * LANES
    assert nq % 2 == 0, "kv blocks are consumed in pairs"
    return pl.pallas_call(
        functools.partial(_attn_kernel, tq=tq, heads=heads, lambda_init=lambda_init),
        grid=(bsz, ATTN_HEADS // heads, nq),
        in_specs=[
            pl.BlockSpec((1, 1, w, tq), lambda b, h, i: (b, i, h, 0)),
            pl.BlockSpec((1, 1, w, tq), lambda b, h, i: (b, jnp.minimum(i + 1, nq - 1), h, 0)),
            pl.BlockSpec((1, seq, w), lambda b, h, i: (b, 0, h)),
            pl.BlockSpec((1, nq, w, tq), lambda b, h, i: (b, 0, h, 0)),
            pl.BlockSpec((4, ATTN_QK_DIM), lambda b, h, i: (0, 0)),
            pl.BlockSpec((ATTN_V_DIM, 1), lambda b, h, i: (0, 0)),
        ],
        out_specs=pl.BlockSpec((1, tq, w), lambda b, h, i: (b, i, h)),
        out_shape=jax.ShapeDtypeStruct((bsz, seq, dv), BF16),
        scratch_shapes=[pltpu.VMEM((heads, ATTN_V_DIM, 2 * tq), F32), pltpu.VMEM((heads, 1, 2 * tq), F32),
                        pltpu.VMEM((heads, 1, 2 * tq), F32), pltpu.VMEM((2, heads, LANES, 2 * tq), BF16),
                        pltpu.VMEM((heads, 2 * tq, 2 * tq), F32), pltpu.VMEM((heads, 2 * tq, 2 * tq), F32),
                        pltpu.SMEM((1,), I32)],
        compiler_params=_cparams(("arbitrary", "arbitrary", "arbitrary")),
        name="diff_attn",
    )(qt4, qt4, k, vt4, lam_vecs, head_gain.reshape(ATTN_V_DIM, 1))


def _ssd_kernel(xbc_ref, dt_ref, z_ref, cw_ref, cb_ref, alog_ref, dskip_ref, nw_ref, expand_ref,
                y_ref, tail_ref, state_ref, *, n_heads):
    c = pl.program_id(1)
    L = xbc_ref.shape[1]
    d_inner = n_heads * SSD_HEAD_DIM
    gw = d_inner // SSD_GROUPS
    hpg = n_heads // SSD_GROUPS

    @pl.when(c == 0)
    def _():
        tail_ref[...] = jnp.zeros_like(tail_ref)
        state_ref[...] = jnp.zeros_like(state_ref)

    u = xbc_ref[0]
    tail = tail_ref[...]
    row8 = lax.broadcasted_iota(I32, tail.shape, 0)
    conv = u * cw_ref[SSD_CONV - 1:SSD_CONV, :] + cb_ref[...]
    for s in range(1, SSD_CONV):
        r = pltpu.roll(u, s, axis=0)
        top = jnp.where(row8 < s, pltpu.roll(tail, s, axis=0), r[:SUBLANES])
        shifted = jnp.concatenate([top, r[SUBLANES:]], axis=0)
        conv = conv + shifted * cw_ref[SSD_CONV - 1 - s:SSD_CONV - s, :]
    tail_ref[...] = u[L - SUBLANES:, :]
    xbc = _silu(conv)
    xs = xbc[:, :d_inner]
    bmat = xbc[:, d_inner:d_inner + SSD_GROUPS * SSD_STATE]
    cmat = xbc[:, d_inner + SSD_GROUPS * SSD_STATE:]

    dt = dt_ref[0]
    a = dt * (-jnp.exp(alog_ref[...]))
    ri = lax.broadcasted_iota(I32, (L, L), 0)
    ci = lax.broadcasted_iota(I32, (L, L), 1)
    causal = ri >= ci
    tril = jnp.where(causal, 1.0, 0.0).astype(BF16)
    a_hi, a_mid, a_lo = _split3(a)
    a_cum = _dot(tril, a_hi) + _dot(tril, a_mid) + _dot(tril, a_lo)
    a_cum_t = a_cum.T
    a_last = a_cum[L - 1:L, :]
    ea = jnp.exp(a_cum)
    to_end = jnp.exp(a_last - a_cum)

    expand = expand_ref[...]

    def widen(v):
        hi, lo = _split2(v)
        return _dot(hi, expand) + _dot(lo, expand)

    dt_w = widen(dt)
    ea_w = widen(ea)
    te_w = widen(to_end)
    xdt = xs * dt_w
    xdt_b = xdt.astype(BF16)
    xte_b = (xdt * te_w).astype(BF16)

    y_parts = []
    for g in range(SSD_GROUPS):
        cg = cmat[:, g * SSD_STATE:(g + 1) * SSD_STATE]
        bg = bmat[:, g * SSD_STATE:(g + 1) * SSD_STATE]
        cg_b = cg.astype(BF16)
        cb = _dot_nt(cg_b, bg.astype(BF16))
        st = state_ref[g]
        y_off = _dot(cg_b, st.astype(BF16)) * ea_w[:, g * gw:(g + 1) * gw]
        y_diag = []
        for e in range(hpg):
            hh = g * hpg + e
            seg = a_cum[:, hh:hh + 1] - a_cum_t[hh:hh + 1, :]
            decay = jnp.exp(jnp.where(causal, seg, -jnp.inf))
            m = (cb * decay).astype(BF16)
            y_diag.append(_dot(m, xdt_b[:, hh * SSD_HEAD_DIM:(hh + 1) * SSD_HEAD_DIM]))
        y_parts.append(jnp.concatenate(y_diag, axis=1) + y_off)
        bt = bg.T.astype(BF16)
        state_ref[g] = st * ea_w[L - 1:L, g * gw:(g + 1) * gw] + _dot(bt, xte_b[:, g * gw:(g + 1) * gw])

    y = jnp.concatenate(y_parts, axis=1) + dskip_ref[...] * xs
    y = y * _silu(z_ref[0])
    outs = []
    for g in range(SSD_GROUPS):
        yg = y[:, g * gw:(g + 1) * gw]
        ms = jnp.mean(yg * yg, axis=-1, keepdims=True)
        outs.append(yg * lax.rsqrt(ms + NORM_EPS) * nw_ref[:, g * gw:(g + 1) * gw])
    y_ref[0] = jnp.concatenate(outs, axis=1).astype(y_ref.dtype)


def _ssd_call(xbc, dt, z, conv_w, conv_b, a_log, d_skip, ssd_norm):
    bsz, seq, d_conv = xbc.shape
    n_heads = a_log.shape[0]
    d_inner = n_heads * SSD_HEAD_DIM
    L = SSD_CHUNK
    alog = jnp.pad(a_log.astype(F32), (0, LANES - n_heads)).reshape(1, LANES)
    dskip = jnp.repeat(d_skip.astype(F32), SSD_HEAD_DIM).reshape(1, d_inner)
    head_of_lane = jnp.arange(d_inner) // SSD_HEAD_DIM
    expand = (jnp.arange(LANES)[:, None] == head_of_lane[None, :]).astype(BF16)
    const2 = lambda b, c: (0, 0)
    return pl.pallas_call(
        functools.partial(_ssd_kernel, n_heads=n_heads),
        grid=(bsz, seq // L),
        in_specs=[
            pl.BlockSpec((1, L, d_conv), lambda b, c: (b, c, 0)),
            pl.BlockSpec((1, L, LANES), lambda b, c: (b, c, 0)),
            pl.BlockSpec((1, L, d_inner), lambda b, c: (b, c, 0)),
            pl.BlockSpec((SSD_CONV, d_conv), const2),
            pl.BlockSpec((1, d_conv), const2),
            pl.BlockSpec((1, LANES), const2),
            pl.BlockSpec((1, d_inner), const2),
            pl.BlockSpec((1, d_inner), const2),
            pl.BlockSpec((LANES, d_inner), const2),
        ],
        out_specs=pl.BlockSpec((1, L, d_inner), lambda b, c: (b, c, 0)),
        out_shape=jax.ShapeDtypeStruct((bsz, seq, d_inner), BF16),
        scratch_shapes=[
            pltpu.VMEM((SUBLANES, d_conv), F32),
            pltpu.VMEM((SSD_GROUPS, SSD_STATE, d_inner // SSD_GROUPS), F32),
        ],
        compiler_params=_cparams(("arbitrary", "arbitrary")),
        name="ssd_scan",
    )(xbc, dt, z, conv_w, conv_b.reshape(1, d_conv), alog, dskip, ssd_norm.reshape(1, d_inner), expand)


def _merge_kernel(x_ref, ya_ref, ys_ref, gates_ref, mod_ref, nf_ref,
                  wba_ref, wbs_ref, wo_ref, wr_ref, wsg_ref, wsu_ref, wsd_ref,
                  h2_ref, base_ref, sc_ref):
    d = x_ref.shape[2]
    gate_m = mod_ref[0, 2:3, :]
    shift_f = mod_ref[0, 3:4, :]
    scale_f = mod_ref[0, 4:5, :]
    gate_f = mod_ref[0, 5:6, :]
    gates = gates_ref[0]
    mixed = gates[:, :d] * _dot(ya_ref[0], wba_ref[...]) + gates[:, d:] * _dot(ys_ref[0], wbs_ref[...])
    x1 = x_ref[0] + gate_m * _dot(mixed.astype(BF16), wo_ref[...])
    ms = jnp.mean(x1 * x1, axis=-1, keepdims=True)
    h2 = (x1 * lax.rsqrt(ms + NORM_EPS) * nf_ref[...]) * (1.0 + scale_f) + shift_f
    h2b = h2.astype(BF16)
    h2_ref[0] = _to_tiles(h2b)
    sc_ref[0] = _sigmoid(_dot(h2b, wr_ref[...]))
    act = _silu(_dot(h2b, wsg_ref[...])) * _dot(h2b, wsu_ref[...])
    shared = _dot(act.astype(BF16), wsd_ref[...])
    base_ref[0] = x1 + gate_f * shared


def _merge_call(x, y_attn, y_ssd, gates, mod3, norm_ffn, w_ba, w_bs, w_out, w_router, w_sg, w_su, w_sd, tm):
    bsz, seq, d = x.shape
    ws = [w.astype(BF16) for w in (w_ba, w_bs, w_out, w_router, w_sg, w_su, w_sd)]
    const2 = lambda b, i: (0, 0)
    wspec = lambda w: pl.BlockSpec(w.shape, const2, pipeline_mode=pl.Buffered(1))
    row_spec = lambda n: pl.BlockSpec((1, tm, n), lambda b, i: (b, i, 0))
    return pl.pallas_call(
        _merge_kernel,
        grid=(bsz, seq // tm),
        in_specs=[row_spec(d), row_spec(y_attn.shape[2]), row_spec(y_ssd.shape[2]), row_spec(2 * d),
                  pl.BlockSpec((1, 6, d), lambda b, i: (b, 0, 0)), pl.BlockSpec((1, d), const2)]
                 + [wspec(w) for w in ws],
        out_specs=[pl.BlockSpec((1, tm, d // LANES, LANES), lambda b, i: (b, i, 0, 0)), row_spec(d),
                   row_spec(N_EXPERTS)],
        out_shape=[jax.ShapeDtypeStruct((bsz, seq, d // LANES, LANES), BF16),
                   jax.ShapeDtypeStruct((bsz, seq, d), F32), jax.ShapeDtypeStruct((bsz, seq, N_EXPERTS), F32)],
        compiler_params=_cparams(("arbitrary", "arbitrary")),
        name="merge_ffn_pre",
    )(x, y_attn, y_ssd, gates, mod3, norm_ffn.reshape(1, d), *ws)


def _first_max(v, iota, sentinel):
    mx = jnp.max(v, axis=0, keepdims=True)
    first = jnp.min(jnp.where(v == mx, iota, sentinel), axis=0, keepdims=True)
    return mx, first


def _route_kernel(sc_ref, bias_ref, idx_ref, w_ref):
    sc = sc_ref[...].T
    tm = sc.shape[1]
    choice = sc + bias_ref[...]
    per_group = N_EXPERTS // N_EXPERT_GROUPS
    gi = lax.broadcasted_iota(I32, (per_group, tm), 0)
    gscore = []
    for g in range(N_EXPERT_GROUPS):
        cg = choice[g * per_group:(g + 1) * per_group, :]
        m1, f1 = _first_max(cg, gi, per_group)
        m2 = jnp.max(jnp.where(gi == f1, -jnp.inf, cg), axis=0, keepdims=True)
        gscore.append(m1 + m2)
    cur = jnp.concatenate(gscore, axis=0)
    giota = lax.broadcasted_iota(I32, cur.shape, 0)
    keep = jnp.zeros(cur.shape, F32)
    for _ in range(TOPK_GROUPS):
        _, f = _first_max(cur, giota, N_EXPERT_GROUPS)
        sel = giota == f
        keep = jnp.where(sel, 1.0, keep)
        cur = jnp.where(sel, -jnp.inf, cur)
    masked = jnp.concatenate(
        [jnp.where(keep[g:g + 1, :] > 0.5, choice[g * per_group:(g + 1) * per_group, :], -jnp.inf)
         for g in range(N_EXPERT_GROUPS)], axis=0)
    eiota = lax.broadcasted_iota(I32, masked.shape, 0)
    ids, ws = [], []
    for _ in range(TOP_K):
        _, f = _first_max(masked, eiota, N_EXPERTS)
        sel = eiota == f
        ids.append(f)
        ws.append(jnp.sum(jnp.where(sel, sc, 0.0), axis=0, keepdims=True))
        masked = jnp.where(sel, -jnp.inf, masked)
    idx_ref[...] = jnp.concatenate(ids, axis=0)
    w = jnp.concatenate(ws, axis=0)
    w = w / (jnp.sum(w, axis=0, keepdims=True) + 1e-20) * ROUTED_SCALE
    cols = [jnp.broadcast_to(w[k:k + 1, :], (LANES, tm)).T for k in range(TOP_K)]
    w_ref[...] = _to_tiles(jnp.concatenate(cols, axis=1))


def _route_call(scores, router_bias, tm):
    n = scores.shape[0]
    return pl.pallas_call(
        _route_kernel,
        grid=(n // tm,),
        in_specs=[pl.BlockSpec((tm, N_EXPERTS), lambda i: (i, 0)), pl.BlockSpec((N_EXPERTS, 1), lambda i: (0, 0))],
        out_specs=[pl.BlockSpec((TOP_K, tm), lambda i: (0, i)), pl.BlockSpec((tm, TOP_K, LANES), lambda i: (i, 0, 0))],
        out_shape=[jax.ShapeDtypeStruct((TOP_K, n), I32), jax.ShapeDtypeStruct((n, TOP_K, LANES), F32)],
        compiler_params=_cparams(("arbitrary",)),
        name="route_topk",
    )(scores, router_bias.astype(F32).reshape(N_EXPERTS, 1))


def _rank_kernel(idx_ref, dest_ref, cnt_ref, base_ref, *, blk):
    phase = pl.program_id(0)
    i = pl.program_id(1)
    idx = idx_ref[...]
    tm = idx.shape[1]
    eiota = lax.broadcasted_iota(I32, (N_EXPERTS, tm), 0)
    onehot = jnp.zeros((N_EXPERTS, tm), F32)
    for k in range(TOP_K):
        onehot = onehot + jnp.where(eiota == idx[k:k + 1, :], 1.0, 0.0)

    @pl.when((phase == 0) & (i == 0))
    def _():
        base_ref[...] = jnp.zeros_like(base_ref)

    @pl.when((phase == 1) & (i == 0))
    def _():
        counts = base_ref[...]
        cnt_ref[...] = counts
        padded = jnp.floor((counts + (blk - 1)) * (1.0 / blk)) * blk
        ri = lax.broadcasted_iota(I32, (N_EXPERTS, N_EXPERTS), 0)
        ci = lax.broadcasted_iota(I32, (N_EXPERTS, N_EXPERTS), 1)
        lower = jnp.where(ci < ri, 1.0, 0.0).astype(BF16)
        hi, mid, lo = _split3(padded)
        base_ref[...] = _dot(lower, hi) + _dot(lower, mid) + _dot(lower, lo)

    @pl.when(phase == 1)
    def _():
        ri = lax.broadcasted_iota(I32, (tm, tm), 0)
        ci = lax.broadcasted_iota(I32, (tm, tm), 1)
        before = jnp.where(ri < ci, 1.0, 0.0).astype(BF16)
        prior = _dot(onehot.astype(BF16), before) + base_ref[:, 0:1]
        rows = []
        for k in range(TOP_K):
            rows.append(jnp.sum(jnp.where(eiota == idx[k:k + 1, :], prior, 0.0), axis=0, keepdims=True))
        dest_ref[...] = jnp.concatenate(rows, axis=0).astype(I32)

    base_ref[...] = base_ref[...] + jnp.sum(onehot, axis=1, keepdims=True)


def _rank_call(idx, tm, blk):
    n = idx.shape[1]
    return pl.pallas_call(
        functools.partial(_rank_kernel, blk=blk),
        grid=(2, n // tm),
        in_specs=[pl.BlockSpec((TOP_K, tm), lambda p, i: (0, i))],
        out_specs=[pl.BlockSpec((TOP_K, tm), lambda p, i: (0, i * p)),
                   pl.BlockSpec((N_EXPERTS, LANES), lambda p, i: (0, 0))],
        out_shape=[jax.ShapeDtypeStruct((TOP_K, n), I32), jax.ShapeDtypeStruct((N_EXPERTS, LANES), F32)],
        scratch_shapes=[pltpu.VMEM((N_EXPERTS, LANES), F32)],
        compiler_params=_cparams(("arbitrary", "arbitrary")),
        name="expert_rank",
    )(idx)


def _dispatch_kernel(pend_ref, padded_ref, nu_ref, dest_ref, h_ref, xs_ref, zero_ref, sem_z, sem_s, *, blk):
    i = pl.program_id(0)
    tm = h_ref.shape[0]
    n_blocks = xs_ref.shape[0] // blk

    def zero_copy(start):
        return pltpu.make_async_copy(zero_ref, xs_ref.at[pl.ds(pl.multiple_of(start, blk), blk)], sem_z)

    @pl.when(i == 0)
    def _():
        zero_ref[...] = jnp.zeros_like(zero_ref)

        def start(e, _):
            @pl.when(padded_ref[e] > 0)
            def _():
                zero_copy(pend_ref[e] - blk).start()
            return 0

        def wait(e, _):
            @pl.when(padded_ref[e] > 0)
            def _():
                zero_copy(pend_ref[e] - blk).wait()
            return 0

        def start_idle(b, _):
            zero_copy(b * blk).start()
            return 0

        def wait_idle(b, _):
            zero_copy(b * blk).wait()
            return 0

        lax.fori_loop(0, N_EXPERTS, start, 0)
        lax.fori_loop(nu_ref[0], n_blocks, start_idle, 0)
        lax.fori_loop(0, N_EXPERTS, wait, 0)
        lax.fori_loop(nu_ref[0], n_blocks, wait_idle, 0)

    def issue(g, _):
        for u in range(ISSUE_UNROLL):
            t = g * ISSUE_UNROLL + u
            for k in range(TOP_K):
                pltpu.make_async_copy(h_ref.at[t], xs_ref.at[dest_ref[k, t]], sem_s).start(priority=k % 2)
        return 0

    lax.fori_loop(0, tm // ISSUE_UNROLL, issue, 0)
    for k in range(TOP_K):
        pltpu.make_async_copy(h_ref, xs_ref.at[pl.ds(0, tm)], sem_s).wait()


def _dispatch_call(pend, padded, n_used, dest, h2t, n_slots, blk, tm):
    n, s, lanes = h2t.shape
    return pl.pallas_call(
        functools.partial(_dispatch_kernel, blk=blk),
        grid_spec=pltpu.PrefetchScalarGridSpec(
            num_scalar_prefetch=3,
            grid=(n // tm,),
            in_specs=[
                pl.BlockSpec((TOP_K, tm), lambda i, pe, pa, nu: (0, i), memory_space=pltpu.SMEM),
                pl.BlockSpec((tm, s, lanes), lambda i, pe, pa, nu: (i, 0, 0)),
            ],
            out_specs=pl.BlockSpec(memory_space=pl.ANY),
            scratch_shapes=[pltpu.VMEM((blk, s, lanes), h2t.dtype), pltpu.SemaphoreType.DMA(()),
                            pltpu.SemaphoreType.DMA(())],
        ),
        out_shape=jax.ShapeDtypeStruct((n_slots, s, lanes), h2t.dtype),
        compiler_params=_cparams(("arbitrary",)),
        name="moe_dispatch",
    )(pend, padded, n_used, dest, h2t)


RING = 3


def _expert_kernel(be_ref, nu_ref, first_ref, wslot_ref, nxt1_ref, nxt2_ref, xs_hbm, wg_hbm, wu_hbm, wd_hbm, y_hbm,
                   xbuf, ybuf, zbuf, wg_f, wu_f, wd_f, wg_b, wu_b, wd_b, x_sems, y_sems, w_sems, *, blk):
    i = pl.program_id(0)
    nu = nu_ref[0]
    last = pl.num_programs(0) - 1

    def x_copy(b):
        slot = b % RING
        return pltpu.make_async_copy(xs_hbm.at[pl.ds(pl.multiple_of(b * blk, blk), blk)], xbuf.at[slot],
                                     x_sems.at[slot])

    def y_copy(b, src):
        return pltpu.make_async_copy(src, y_hbm.at[pl.ds(pl.multiple_of(b * blk, blk), blk)], y_sems.at[b % RING])

    def weight_copies(e, slot):
        return (pltpu.make_async_copy(wg_hbm.at[e], wg_f.at[slot], w_sems.at[slot]),
                pltpu.make_async_copy(wu_hbm.at[e], wu_f.at[slot], w_sems.at[slot]),
                pltpu.make_async_copy(wd_hbm.at[e], wd_f.at[slot], w_sems.at[slot]))

    @pl.when(i == 0)
    def _():
        zbuf[...] = jnp.zeros_like(zbuf)
        x_copy(0).start()

        @pl.when(nu > 1)
        def _():
            x_copy(1).start()

        for cp in weight_copies(be_ref[0], 0):
            cp.start()

        @pl.when(nxt1_ref[0] >= 0)
        def _():
            for cp in weight_copies(nxt1_ref[0], 1):
                cp.start()

    @pl.when(i < nu)
    def _():
        @pl.when(i + 2 < nu)
        def _():
            x_copy(i + 2).start()

        @pl.when(first_ref[i] == 1)
        def _():
            slot = wslot_ref[i]

            @pl.when(nxt2_ref[i] >= 0)
            def _():
                for cp in weight_copies(nxt2_ref[i], (slot + 2) % RING):
                    cp.start()

            for cp in weight_copies(be_ref[i], slot):
                cp.wait()
            wg_b[...] = wg_f[slot].astype(BF16)
            wu_b[...] = wu_f[slot].astype(BF16)
            wd_b[...] = wd_f[slot].astype(BF16)

        x_copy(i).wait()
        slot = i % RING
        half = blk // 2
        xs_h = [_from_tiles(xbuf[slot, r * half:(r + 1) * half]) for r in range(2)]
        gu = [(_dot(xh, wg_b[...]), _dot(xh, wu_b[...])) for xh in xs_h]
        for r in range(2):
            act = _silu(gu[r][0]) * gu[r][1]
            ybuf[slot, r * half:(r + 1) * half] = _to_tiles(_dot(act.astype(BF16), wd_b[...]).astype(ybuf.dtype))
        y_copy(i, ybuf.at[slot]).start()

    @pl.when(i >= nu)
    def _():
        y_copy(i, zbuf).start()

    @pl.when(i >= 2)
    def _():
        y_copy(i - 2, zbuf).wait()

    @pl.when(i == last)
    def _():
        @pl.when(i >= 1)
        def _():
            y_copy(i - 1, zbuf).wait()

        y_copy(i, zbuf).wait()


def _expert_call(block_e, n_used, first, wslot, nxt1, nxt2, xs, w_gate, w_up, w_down, blk):
    n_slots, s, lanes = xs.shape
    _, d, ff = w_gate.shape
    n_blocks = n_slots // blk
    hbm = pl.BlockSpec(memory_space=pl.ANY)
    return pl.pallas_call(
        functools.partial(_expert_kernel, blk=blk),
        grid_spec=pltpu.PrefetchScalarGridSpec(
            num_scalar_prefetch=6,
            grid=(n_blocks,),
            in_specs=[hbm, hbm, hbm, hbm],
            out_specs=hbm,
            scratch_shapes=[pltpu.VMEM((RING, blk, s, lanes), xs.dtype), pltpu.VMEM((RING, blk, s, lanes), BF16),
                            pltpu.VMEM((blk, s, lanes), BF16),
                            pltpu.VMEM((RING, d, ff), F32), pltpu.VMEM((RING, d, ff), F32),
                            pltpu.VMEM((RING, ff, d), F32),
                            pltpu.VMEM((d, ff), BF16), pltpu.VMEM((d, ff), BF16), pltpu.VMEM((ff, d), BF16),
                            pltpu.SemaphoreType.DMA((RING,)), pltpu.SemaphoreType.DMA((RING,)),
                            pltpu.SemaphoreType.DMA((RING,))],
        ),
        out_shape=jax.ShapeDtypeStruct((n_slots, s, lanes), BF16),
        compiler_params=_cparams(("arbitrary",)),
        name="moe_experts",
    )(block_e, n_used, first, wslot, nxt1, nxt2, xs, w_gate, w_up, w_down)


def _combine_kernel(dest_ref, dnext_ref, w_ref, base_ref, gf_ref, nfin_ref, y_ref, o_ref, buf_ref, sems, *,
                    final_norm):
    i = pl.program_id(0)
    last = pl.num_programs(0) - 1
    tc = base_ref.shape[0]

    def issue_all(d_ref, slot):
        def issue(g, _):
            for u in range(ISSUE_UNROLL):
                t = g * ISSUE_UNROLL + u
                for k in range(TOP_K):
                    pltpu.make_async_copy(y_ref.at[d_ref[k, t]], buf_ref.at[slot, k, t],
                                          sems.at[slot]).start(priority=k % 2)
            return 0

        lax.fori_loop(0, tc // ISSUE_UNROLL, issue, 0)

    @pl.when(i == 0)
    def _():
        issue_all(dest_ref, 0)

    for par in range(2):
        @pl.when((i < last) & (i % 2 == par))
        def _():
            issue_all(dnext_ref, 1 - par)

    for par in range(2):
        @pl.when(i % 2 == par)
        def _():
            for k in range(TOP_K):
                pltpu.make_async_copy(y_ref.at[pl.ds(0, tc)], buf_ref.at[par, k], sems.at[par]).wait()

    slot = i % 2
    routed = buf_ref[slot, 0].astype(F32) * w_ref[:, 0:1, :]
    for k in range(1, TOP_K):
        routed = routed + buf_ref[slot, k].astype(F32) * w_ref[:, k:k + 1, :]
    x2 = base_ref[...] + gf_ref[0] * _from_tiles(routed)
    if final_norm:
        ms = jnp.mean(x2 * x2, axis=-1, keepdims=True)
        x2 = x2 * lax.rsqrt(ms + NORM_EPS) * nfin_ref[...]
    o_ref[...] = x2


def _combine_call(dest, w_tok, base, gate_f, norm_final, y, tokens_per_batch, tc, final_norm):
    n, d = base.shape
    steps_per_batch = tokens_per_batch // tc
    steps = n // tc
    return pl.pallas_call(
        functools.partial(_combine_kernel, final_norm=final_norm),
        grid=(steps,),
        in_specs=[
            pl.BlockSpec((TOP_K, tc), lambda i: (0, i), memory_space=pltpu.SMEM),
            pl.BlockSpec((TOP_K, tc), lambda i: (0, jnp.minimum(i + 1, steps - 1)), memory_space=pltpu.SMEM),
            pl.BlockSpec((tc, TOP_K, LANES), lambda i: (i, 0, 0)),
            pl.BlockSpec((tc, d), lambda i: (i, 0)),
            pl.BlockSpec((1, 1, d), lambda i: (i // steps_per_batch, 0, 0)),
            pl.BlockSpec((1, d), lambda i: (0, 0)),
            pl.BlockSpec(memory_space=pl.ANY),
        ],
        out_specs=pl.BlockSpec((tc, d), lambda i: (i, 0)),
        out_shape=jax.ShapeDtypeStruct((n, d), F32),
        scratch_shapes=[pltpu.VMEM((2, TOP_K, tc, d // LANES, LANES), y.dtype), pltpu.SemaphoreType.DMA((2,))],
        compiler_params=_cparams(("arbitrary",)),
        name="moe_combine",
    )(dest, dest, w_tok, base, gate_f, norm_final.reshape(1, d), y)


def _moe_blocks(n_pairs, blk):
    return -(-n_pairs // blk) + N_EXPERTS


def _moe_layout(counts, n_pairs, blk):
    n_blocks = _moe_blocks(n_pairs, blk)
    padded = (counts + blk - 1) // blk * blk
    pend = jnp.cumsum(padded)
    n_used = pend[-1] // blk
    blk_start = jnp.arange(n_blocks, dtype=I32) * blk
    block_e = jnp.sum((pend[None, :] <= blk_start[:, None]).astype(I32), axis=1)
    block_e = jnp.minimum(block_e, N_EXPERTS - 1)
    used = jnp.arange(n_blocks) < n_used
    prev_e = jnp.concatenate([jnp.full((1,), -1, I32), block_e[:-1]])
    first = (used & (block_e != prev_e)).astype(I32)
    has = padded > 0
    ids = jnp.arange(N_EXPERTS, dtype=I32)
    ordinal = jnp.cumsum(has.astype(I32)) - 1
    at_or_after = lax.cummin(jnp.where(has, ids, N_EXPERTS), axis=0, reverse=True)
    after = jnp.concatenate([at_or_after[1:], jnp.full((2,), N_EXPERTS, I32)])
    nxt1_e = after[:N_EXPERTS]
    nxt2_e = after[jnp.minimum(nxt1_e, N_EXPERTS)]
    fix = lambda v: jnp.where(v >= N_EXPERTS, -1, v).astype(I32)
    wslot = (ordinal % RING).astype(I32)[block_e]
    nxt1 = fix(nxt1_e)[block_e]
    nxt2 = fix(nxt2_e)[block_e]
    return (pend.astype(I32), padded.astype(I32), block_e.astype(I32), n_used.reshape(1).astype(I32),
            first, wslot, nxt1, nxt2, n_blocks)


def kernel(x, c, positions, w_ada, b_ada, norm_mix, w_in, b_gate, lambda_q1, lambda_k1, lambda_q2, lambda_k2,
           attn_head_norm, conv_w, conv_b, dt_bias, a_log, d_skip, ssd_norm, w_branch_attn, w_branch_ssd, w_out,
           norm_ffn, w_router, router_bias, w_exp_gate, w_exp_up, w_exp_down, w_sh_gate, w_sh_up, w_sh_down,
           norm_final):
    bsz, seq, d = x.shape
    n = bsz * seq
    depth = w_ada.shape[0]
    tm = min(TOKEN_TILE, seq)
    c_pad = jnp.pad(c, ((0, -bsz % SUBLANES), (0, 0)))
    for l in range(depth):
        last = l == depth - 1
        mod3 = _ada_call(c_pad, w_ada[l], b_ada[l])[:bsz].reshape(bsz, 6, d)
        qt4, k, vt4, z, xbc, gates, dt = _inproj_call(x, positions, mod3, norm_mix[l], b_gate[l], dt_bias[l],
                                                      w_in[l], tm)
        lambda_init = 0.8 - 0.6 * math.exp(-0.3 * l)
        lam_vecs = jnp.stack([lambda_q1[l], lambda_k1[l], lambda_q2[l], lambda_k2[l]]).astype(F32)
        y_attn = _attn_call(qt4, k, vt4, lam_vecs, attn_head_norm[l].astype(F32), lambda_init, ATTN_HEADS_PER_STEP)
        y_ssd = _ssd_call(xbc, dt, z, conv_w[l], conv_b[l], a_log[l], d_skip[l], ssd_norm[l])
        h2, base, scores = _merge_call(x, y_attn, y_ssd, gates, mod3, norm_ffn[l], w_branch_attn[l],
                                       w_branch_ssd[l], w_out[l], w_router[l], w_sh_gate[l], w_sh_up[l],
                                       w_sh_down[l], min(MERGE_TILE, seq))
        rt = min(ROUTE_TILE, n)
        idx, w_tiles = _route_call(scores.reshape(n, N_EXPERTS), router_bias[l], rt)
        dest, cnt = _rank_call(idx, rt, MOE_BLOCK)
        pend, padded, block_e, n_used, first, wslot, nxt1, nxt2, n_blocks = _moe_layout(
            cnt[:, 0].astype(I32), n * TOP_K, MOE_BLOCK)
        xs = _dispatch_call(pend, padded, n_used, dest, h2.reshape(n, d // LANES, LANES), n_blocks * MOE_BLOCK,
                            MOE_BLOCK, min(DISPATCH_TILE, seq))
        y = _expert_call(block_e, n_used, first, wslot, nxt1, nxt2, xs, w_exp_gate[l], w_exp_up[l], w_exp_down[l],
                         MOE_BLOCK)
        out = _combine_call(dest, w_tiles, base.reshape(n, d), mod3[:, 5:6, :], norm_final, y, seq,
                            min(COMBINE_TILE, seq), final_norm=last)
        x = out.reshape(bsz, seq, d)
    return x
```
